```python
import jax, jax.numpy as jnp
from jax import lax
import numpy as np

D_MODEL = 4096
BATCH = 8
SEQ = 2048
DEPTH = 4
DEC_BATCH = 4
DEC_SEQ = 2048
PAST_LEN = 128

HEAD_DIM = 128
N_MIXERS = 3
EPS = 1e-6
NEG_INF = -1e30
N_MEM = 256
MEM_HEADS = 4
MEM_WIDTH = MEM_HEADS * HEAD_DIM
D_FF = 4 * D_MODEL
A_CHUNK = 128
A_GROUPS = 8
A_WIDTH = D_MODEL
B_GROUPS = 4
B_WIDTH = D_MODEL
C_PAIRS = ((128, 1), (512, 4), (2048, 16))
C_GROUPS = len(C_PAIRS)
C_HEADS = 8
C_WIDTH = C_HEADS * HEAD_DIM
ALIBI_MAX = 8.0

N_A_LAYERS = (DEPTH + 2) // 3
N_B_LAYERS = (DEPTH + 1) // 3
N_C_LAYERS = DEPTH // 3

IN_A = 2 * A_WIDTH + MEM_WIDTH
OUT_A = A_WIDTH + MEM_WIDTH
IN_B = B_WIDTH + MEM_WIDTH
OUT_B = B_WIDTH + MEM_WIDTH
IN_C = 3 * C_GROUPS * C_WIDTH + MEM_WIDTH
OUT_C = C_WIDTH + MEM_WIDTH

kernel_name = "hybrid_gmlp_fnet_dilated_encoder"


def rmsnorm(x, g):
    xf = x.astype(jnp.float32)
    y = xf * lax.rsqrt(jnp.mean(xf * xf, axis=-1, keepdims=True) + EPS)
    return (y * g.astype(jnp.float32)).astype(x.dtype)


def alibi_slopes(n):
    return jnp.asarray(2.0 ** (-ALIBI_MAX * np.arange(1, n + 1) / n), dtype=jnp.float32)


def gated_chunk_mlp(z, w_s, b_s, v_gain):
    b, s, _ = z.shape
    z = jax.nn.gelu(z, approximate=False)
    u, v = jnp.split(z, 2, axis=-1)
    v = rmsnorm(v, v_gain)
    v = v.reshape(b, s // A_CHUNK, A_CHUNK, A_GROUPS, A_WIDTH // A_GROUPS)
    vm = jnp.einsum('gts,bnsgc->bntgc', w_s, v) + b_s.T[None, None, :, :, None]
    return u * vm.reshape(b, s, A_WIDTH)


def fourier_mix(z):
    b, s, _ = z.shape
    zg = z.reshape(b, s, B_GROUPS, B_WIDTH // B_GROUPS).astype(jnp.float32)
    f = jnp.fft.fft2(zg, axes=(1, 3), norm="ortho").real
    return f.astype(z.dtype).reshape(b, s, B_WIDTH)


def dilated_window_attention(q, k, v, dilation, radius, slopes):
    b, s, h, dh = q.shape
    d = dilation
    L = -(-s // d)
    L = -(-L // radius) * radius
    nb = L // radius
    pad = L * d - s

    def to_blocks(t):
        t = jnp.pad(t, ((0, 0), (0, pad), (0, 0), (0, 0)))
        t = t.reshape(b, L, d, h, dh).transpose(0, 2, 1, 3, 4)
        return t.reshape(b, d, nb, radius, h, dh)

    def neighbours(t):
        tp = jnp.pad(t, ((0, 0), (0, 0), (1, 1), (0, 0), (0, 0), (0, 0)))
        return jnp.concatenate([tp[:, :, :-2], tp[:, :, 1:-1], tp[:, :, 2:]], axis=3)

    qb = to_blocks(q)
    kn = neighbours(to_blocks(k))
    vn = neighbours(to_blocks(v))
    scores = jnp.einsum('bdnqhc,bdnkhc->bdnhqk', qb, kn).astype(jnp.float32) * (HEAD_DIM ** -0.5)

    jq = jnp.arange(nb)[:, None] * radius + jnp.arange(radius)[None, :]
    jk = jnp.arange(nb)[:, None] * radius - radius + jnp.arange(3 * radius)[None, :]
    pos_k = jk[None] * d + jnp.arange(d)[:, None, None]
    dist = jnp.abs(jq[:, :, None] - jk[:, None, :])
    valid = (dist <= radius)[None] & (jk >= 0)[None, :, None, :] & (pos_k < s)[:, :, None, :]
    bias = -slopes[None, :, None, None] * (dist * d).astype(jnp.float32)[:, None]
    logits = jnp.where(valid[None, :, :, None], scores + bias, NEG_INF)
    m = jnp.max(logits, axis=-1, keepdims=True)
    p = jnp.exp(logits - m)
    den = jnp.sum(p, axis=-1, keepdims=True)
    out = jnp.einsum('bdnhqk,bdnkhc->bdnqhc', (p / den).astype(v.dtype), vn)
    lse = (m + jnp.log(den))[..., 0]
    out = out.reshape(b, d, L, h, dh).transpose(0, 2, 1, 3, 4).reshape(b, L * d, h, dh)[:, :s]
    lse = lse.transpose(0, 1, 2, 4, 3).reshape(b, d, L, h).transpose(0, 2, 1, 3).reshape(b, L * d, h)[:, :s]
    return out, lse


def dilated_mixture(z, q_gain, k_gain):
    b, s, _ = z.shape
    zc = z.reshape(b, s, C_GROUPS, 3, C_HEADS, HEAD_DIM)
    slopes = alibi_slopes(C_HEADS)
    outs, lses = [], []
    for g, (window, dilation) in enumerate(C_PAIRS):
        radius = window // (2 * dilation)
        q = rmsnorm(zc[:, :, g, 0], q_gain[g])
        k = rmsnorm(zc[:, :, g, 1], k_gain[g])
        o, l = dilated_window_attention(q, k, zc[:, :, g, 2], dilation, radius, slopes)
        outs.append(o)
        lses.append(l)
    outs = jnp.stack(outs)
    w = jax.nn.softmax(jnp.stack(lses), axis=0)
    y = jnp.einsum('gbsh,gbshc->bshc', w.astype(outs.dtype), outs)
    return y.reshape(b, s, C_WIDTH)


def memory_attention(q, mem_n, w_kv, q_gain, k_gain):
    b, s, _ = q.shape
    kv = mem_n @ w_kv
    k, v = jnp.split(kv, 2, axis=-1)
    q = rmsnorm(q.reshape(b, s, MEM_HEADS, HEAD_DIM), q_gain)
    k = rmsnorm(k.reshape(b, -1, MEM_HEADS, HEAD_DIM), k_gain)
    v = v.reshape(b, -1, MEM_HEADS, HEAD_DIM)
    scores = jnp.einsum('bshc,bmhc->bhsm', q, k).astype(jnp.float32) * (HEAD_DIM ** -0.5)
    p = jax.nn.softmax(scores, axis=-1).astype(v.dtype)
    return jnp.einsum('bhsm,bmhc->bshc', p, v).reshape(b, s, MEM_WIDTH)


def trunk(x, mem, p):
    for i in range(DEPTH):
        kind = i % N_MIXERS
        j = i // N_MIXERS
        h = rmsnorm(x, p['mixer_norm'][i])
        mem_n = rmsnorm(mem, p['mem_norm'][i])
        if kind == 0:
            z = h @ p['a_w_in'][j]
            y = gated_chunk_mlp(z[..., :2 * A_WIDTH], p['a_w_s'][j], p['a_b_s'][j], p['a_v_gain'][j])
            q_mem = z[..., 2 * A_WIDTH:]
            w_out = p['a_w_out'][j]
        elif kind == 1:
            z = h @ p['b_w_in'][j]
            y = fourier_mix(z[..., :B_WIDTH])
            q_mem = z[..., B_WIDTH:]
            w_out = p['b_w_out'][j]
        else:
            z = h @ p['c_w_in'][j]
            y = dilated_mixture(z[..., :3 * C_GROUPS * C_WIDTH], p['c_q_gain'][j], p['c_k_gain'][j])
            q_mem = z[..., 3 * C_GROUPS * C_WIDTH:]
            w_out = p['c_w_out'][j]
        m = memory_attention(q_mem, mem_n, p['mem_w_kv'][i], p['mem_q_gain'][i], p['mem_k_gain'][i])
        x = x + jnp.concatenate([y, m], axis=-1) @ w_out
        h = rmsnorm(x, p['ffn_norm'][i])
        x = x + jnp.square(jax.nn.relu(h @ p['w_ff1'][i])) @ p['w_ff2'][i]
    return x


def setup_inputs(seed: int = 0) -> dict:
    key = jax.random.key(seed)
    ks = iter(jax.random.split(key, 32))
    f32 = jnp.float32

    def w(shape, fan_in):
        return jax.random.normal(next(ks), shape, f32) * (fan_in ** -0.5)

    def gain(shape):
        return 1.0 + 0.02 * jax.random.normal(next(ks), shape, f32)

    return {
        "x_prompt": jax.random.normal(next(ks), (BATCH, SEQ, D_MODEL), f32),
        "x_sample": jax.random.normal(next(ks), (DEC_BATCH, DEC_SEQ, D_MODEL), f32),
        "mem_prompt": jax.random.normal(next(ks), (BATCH, N_MEM, D_MODEL), f32),
        "mem_sample": jax.random.normal(next(ks), (DEC_BATCH, N_MEM, D_MODEL), f32),
        "mixer_norm": gain((DEPTH, D_MODEL)),
        "mem_norm": gain((DEPTH, D_MODEL)),
        "ffn_norm": gain((DEPTH, D_MODEL)),
        "mem_w_kv": w((DEPTH, D_MODEL, 2 * MEM_WIDTH), D_MODEL),
        "mem_q_gain": gain((DEPTH, HEAD_DIM)),
        "mem_k_gain": gain((DEPTH, HEAD_DIM)),
        "w_ff1": w((DEPTH, D_MODEL, D_FF), D_MODEL),
        "w_ff2": w((DEPTH, D_FF, D_MODEL), D_FF),
        "a_w_in": w((N_A_LAYERS, D_MODEL, IN_A), D_MODEL),
        "a_w_out": w((N_A_LAYERS, OUT_A, D_MODEL), OUT_A),
        "a_w_s": w((N_A_LAYERS, A_GROUPS, A_CHUNK, A_CHUNK), A_CHUNK),
        "a_b_s": 1.0 + 0.1 * jax.random.normal(next(ks), (N_A_LAYERS, A_GROUPS, A_CHUNK), f32),
        "a_v_gain": gain((N_A_LAYERS, A_WIDTH)),
        "b_w_in": w((N_B_LAYERS, D_MODEL, IN_B), D_MODEL),
        "b_w_out": w((N_B_LAYERS, OUT_B, D_MODEL), OUT_B),
        "c_w_in": w((N_C_LAYERS, D_MODEL, IN_C), D_MODEL),
        "c_w_out": w((N_C_LAYERS, OUT_C, D_MODEL), OUT_C),
        "c_q_gain": gain((N_C_LAYERS, C_GROUPS, HEAD_DIM)),
        "c_k_gain": gain((N_C_LAYERS, C_GROUPS, HEAD_DIM)),
    }


def reference(x_prompt, x_sample, mem_prompt, mem_sample, mixer_norm, mem_norm, ffn_norm, mem_w_kv,
              mem_q_gain, mem_k_gain, w_ff1, w_ff2, a_w_in, a_w_out, a_w_s, a_b_s, a_v_gain,
              b_w_in, b_w_out, c_w_in, c_w_out, c_q_gain, c_k_gain):
    params = {
        "mixer_norm": mixer_norm, "mem_norm": mem_norm, "ffn_norm": ffn_norm, "mem_w_kv": mem_w_kv,
        "mem_q_gain": mem_q_gain, "mem_k_gain": mem_k_gain, "w_ff1": w_ff1, "w_ff2": w_ff2,
        "a_w_in": a_w_in, "a_w_out": a_w_out, "a_w_s": a_w_s, "a_b_s": a_b_s, "a_v_gain": a_v_gain,
        "b_w_in": b_w_in, "b_w_out": b_w_out, "c_w_in": c_w_in, "c_w_out": c_w_out,
        "c_q_gain": c_q_gain, "c_k_gain": c_k_gain,
    }
    y_prompt = trunk(x_prompt, mem_prompt, params)
    y_sample = trunk(x_sample, mem_sample, params)
    return (y_prompt, y_sample)
```

```python
import functools
import math

import jax
import jax.numpy as jnp
from jax import lax
from jax.experimental import pallas as pl
from jax.experimental.pallas import tpu as pltpu

F32 = jnp.float32
BF16 = jnp.bfloat16

EPS = 1e-6
NEG_INF = -1e30
HEAD_DIM = 128
MEM_HEADS = 4
MEM_WIDTH = MEM_HEADS * HEAD_DIM
A_CHUNK = 128
A_GROUPS = 8
B_GROUPS = 4
C_PAIRS = ((128, 1), (512, 4), (2048, 16))
C_HEADS = 8
C_WIDTH = C_HEADS * HEAD_DIM
ALIBI_MAX = 8.0
N_MIXERS = 3

V7X_VMEM_BYTES = 64 * 1024 * 1024
VMEM_LIMIT_BYTES = V7X_VMEM_BYTES - 6 * 1024 * 1024
V7X_LANES = 128


def _params(*semantics):
    return pltpu.CompilerParams(dimension_semantics=semantics, vmem_limit_bytes=VMEM_LIMIT_BYTES)


def _pick(n, candidates):
    for c in candidates:
        if n % c == 0:
            return c
    raise ValueError(f"no block size in {candidates} divides {n}")


def _rmsnorm_kernel(x_ref, g_ref, o_ref):
    x = x_ref[...].astype(F32)
    ms = jnp.mean(x * x, axis=-1, keepdims=True)
    o_ref[...] = ((x * lax.rsqrt(ms + EPS)) * g_ref[...]).astype(o_ref.dtype)


def rmsnorm(x, g, out_dtype=BF16):
    m, d = x.shape
    bm = _pick(m, (512, 256, 128, 8))
    return pl.pallas_call(
        _rmsnorm_kernel,
        grid=(m // bm,),
        in_specs=[pl.BlockSpec((bm, d), lambda i: (i, 0)),
                  pl.BlockSpec((1, d), lambda i: (0, 0))],
        out_specs=pl.BlockSpec((bm, d), lambda i: (i, 0)),
        out_shape=jax.ShapeDtypeStruct((m, d), out_dtype),
        compiler_params=_params("parallel"),
        name="rmsnorm",
    )(x, g.reshape(1, d).astype(F32))


def _mm_kernel(*refs, n_ops, relu2, has_res):
    a_refs = refs[:n_ops]
    w_refs = refs[n_ops:2 * n_ops]
    res_ref = refs[2 * n_ops] if has_res else None
    o_ref = refs[-1]
    acc = None
    for a_ref, w_ref in zip(a_refs, w_refs):
        d = jnp.dot(a_ref[...], w_ref[...], preferred_element_type=F32)
        acc = d if acc is None else acc + d
    if relu2:
        r = jnp.maximum(acc, 0.0)
        acc = r * r
    if has_res:
        acc = res_ref[...] + acc
    o_ref[...] = acc.astype(o_ref.dtype)


def _mm_blocks(m, n, k_total, out_bytes, has_res):
    for bm, bn in ((1024, 1024), (1024, 512), (512, 1024), (512, 512), (256, 512)):
        if m % bm or n % bn:
            continue
        windows = 2 * (bm * k_total * 2 + k_total * bn * 2 + bm * bn * (out_bytes + 4 * has_res))
        if windows + 3 * bm * bn * 4 <= VMEM_LIMIT_BYTES:
            return bm, bn
    raise ValueError(f"no matmul blocks fit m={m} n={n} k={k_total}")


def matmul(a_list, w_list, *, out_dtype, relu2=False, res=None, name="matmul"):
    m = a_list[0].shape[0]
    n = w_list[0].shape[1]
    bm, bn = _mm_blocks(m, n, sum(a.shape[1] for a in a_list), jnp.dtype(out_dtype).itemsize, res is not None)
    in_specs = [pl.BlockSpec((bm, a.shape[1]), lambda i, j: (i, 0)) for a in a_list]
    in_specs += [pl.BlockSpec((w.shape[0], bn), lambda i, j: (0, j)) for w in w_list]
    args = list(a_list) + list(w_list)
    if res is not None:
        in_specs.append(pl.BlockSpec((bm, bn), lambda i, j: (i, j)))
        args.append(res)
    return pl.pallas_call(
        functools.partial(_mm_kernel, n_ops=len(a_list), relu2=relu2, has_res=res is not None),
        grid=(m // bm, n // bn),
        in_specs=in_specs,
        out_specs=pl.BlockSpec((bm, bn), lambda i, j: (i, j)),
        out_shape=jax.ShapeDtypeStruct((m, n), out_dtype),
        compiler_params=_params("parallel", "parallel"),
        name=name,
    )(*args)


def _mm_kred_kernel(a_ref, w_ref, res_ref, o_ref):
    @pl.when(pl.program_id(2) == 0)
    def _():
        o_ref[...] = res_ref[...]

    o_ref[...] += jnp.dot(a_ref[...], w_ref[...], preferred_element_type=F32)


def matmul_kred_res(a, w, res, *, name="matmul_kred"):
    m, k = a.shape
    n = w.shape[1]
    bm = _pick(m, (1024, 512, 256))
    bn = _pick(n, (1024, 512))
    bk = _pick(k, (2048, 1024))
    return pl.pallas_call(
        _mm_kred_kernel,
        grid=(m // bm, n // bn, k // bk),
        in_specs=[pl.BlockSpec((bm, bk), lambda i, j, kk: (i, kk)),
                  pl.BlockSpec((bk, bn), lambda i, j, kk: (kk, j)),
                  pl.BlockSpec((bm, bn), lambda i, j, kk: (i, j))],
        out_specs=pl.BlockSpec((bm, bn), lambda i, j, kk: (i, j)),
        out_shape=jax.ShapeDtypeStruct((m, n), F32),
        compiler_params=_params("parallel", "parallel", "arbitrary"),
        name=name,
    )(a, w, res)


def _gelu(x):
    return 0.5 * x * (1.0 + lax.erf(x * (1.0 / math.sqrt(2.0))))


def _gate_kernel(zu_ref, zv_ref, vg_ref, ws_ref, bs_ref, o_ref, v_scr, *, n_chunks, gw):
    rows = zu_ref.shape[0]

    def gelu_v(g, ssq):
        c0 = pl.multiple_of(g * gw, gw)
        v = _gelu(zv_ref[:, pl.ds(c0, gw)].astype(F32))
        v_scr[:, pl.ds(c0, gw)] = v
        return ssq + jnp.sum(v * v, axis=-1, keepdims=True)

    ssq = lax.fori_loop(0, A_GROUPS, gelu_v, jnp.zeros((rows, 1), F32))
    inv = lax.rsqrt(ssq / (A_GROUPS * gw) + EPS)

    def gate(g, carry):
        c0 = pl.multiple_of(g * gw, gw)
        vn = ((v_scr[:, pl.ds(c0, gw)] * inv) * vg_ref[:, pl.ds(c0, gw)]).astype(BF16)
        u = _gelu(zu_ref[:, pl.ds(c0, gw)].astype(F32))
        w = ws_ref[g]
        b = bs_ref[g]
        for c in range(n_chunks):
            r0 = c * A_CHUNK
            vm = jnp.dot(w, vn[r0:r0 + A_CHUNK], preferred_element_type=F32) + b
            o_ref[r0:r0 + A_CHUNK, pl.ds(c0, gw)] = (u[r0:r0 + A_CHUNK] * vm).astype(o_ref.dtype)
        return carry

    lax.fori_loop(0, A_GROUPS, gate, 0)


def gated_chunk_mlp(z, w_s, b_s, v_gain):
    m, w2 = z.shape
    width = w2 // 2
    gw = width // A_GROUPS
    rows = _pick(m, (256, 128))
    return pl.pallas_call(
        functools.partial(_gate_kernel, n_chunks=rows // A_CHUNK, gw=gw),
        grid=(m // rows,),
        in_specs=[pl.BlockSpec((rows, width), lambda i: (i, 0)),
                  pl.BlockSpec((rows, width), lambda i: (i, 1)),
                  pl.BlockSpec((1, width), lambda i: (0, 0)),
                  pl.BlockSpec((A_GROUPS, A_CHUNK, A_CHUNK), lambda i: (0, 0, 0)),
                  pl.BlockSpec((A_GROUPS, A_CHUNK, 1), lambda i: (0, 0, 0))],
        out_specs=pl.BlockSpec((rows, width), lambda i: (i, 0)),
        out_shape=jax.ShapeDtypeStruct((m, width), BF16),
        scratch_shapes=[pltpu.VMEM((rows, width), F32)],
        compiler_params=_params("parallel"),
        name="gated_chunk_mlp",
    )(z, z, v_gain.reshape(1, width).astype(F32), w_s.astype(BF16),
      b_s.reshape(A_GROUPS, A_CHUNK, 1).astype(F32))


def _dft_tables(n):
    idx = lax.iota(jnp.int32, n)
    prod = (idx[:, None] * idx[None, :]) % n
    theta = prod.astype(F32) * (2.0 * math.pi / n)
    scale = 1.0 / math.sqrt(n)
    return jnp.cos(theta) * scale, -jnp.sin(theta) * scale


def _chan_dft_kernel(z_ref, w_ref, o_ref):
    o_ref[...] = jnp.dot(z_ref[...], w_ref[...], preferred_element_type=F32).astype(o_ref.dtype)


def _pos_dft_kernel(c_ref, s_ref, pc_ref, ps_ref, o_ref):
    acc = jnp.dot(c_ref[...], pc_ref[...], preferred_element_type=F32)
    acc = acc + jnp.dot(s_ref[...], ps_ref[...], preferred_element_type=F32)
    o_ref[...] = acc.astype(o_ref.dtype)


def fourier_mix(z, n_seq, seq):
    m, width = z.shape
    gc = width // B_GROUPS
    cc, sc = _dft_tables(gc)
    w_ch = jnp.concatenate([cc, -sc], axis=1).astype(BF16)
    cp, sp = _dft_tables(seq)
    cp = cp.astype(BF16)
    nsp = sp.astype(BF16)
    bm = _pick(m, (1024, 512, 256))
    p = pl.pallas_call(
        _chan_dft_kernel,
        grid=(m // bm, B_GROUPS),
        in_specs=[pl.BlockSpec((bm, gc), lambda i, g: (i, g)),
                  pl.BlockSpec((gc, 2 * gc), lambda i, g: (0, 0))],
        out_specs=pl.BlockSpec((bm, 2 * gc), lambda i, g: (i, g)),
        out_shape=jax.ShapeDtypeStruct((m, 2 * width), BF16),
        compiler_params=_params("parallel", "parallel"),
        name="fourier_channels",
    )(z, w_ch)
    bo = _pick(seq, (1024, 512, 256))
    nb = seq // bo
    return pl.pallas_call(
        _pos_dft_kernel,
        grid=(n_seq, B_GROUPS, nb),
        in_specs=[pl.BlockSpec((bo, seq), lambda b, g, i: (i, 0)),
                  pl.BlockSpec((bo, seq), lambda b, g, i: (i, 0)),
                  pl.BlockSpec((seq, gc), lambda b, g, i: (b, 2 * g)),
                  pl.BlockSpec((seq, gc), lambda b, g, i: (b, 2 * g + 1))],
        out_specs=pl.BlockSpec((bo, gc), lambda b, g, i: (b * nb + i, g)),
        out_shape=jax.ShapeDtypeStruct((m, width), BF16),
        compiler_params=_params("parallel", "parallel", "arbitrary"),
        name="fourier_positions",
    )(cp, nsp, p, p)


def _head_norm(x, g):
    x = x.astype(F32)
    ms = jnp.mean(x * x, axis=-1, keepdims=True)
    return ((x * lax.rsqrt(ms + EPS)) * g).astype(BF16)


def _dil_attn_kernel(q_ref, k_ref, v_ref, qg_ref, kg_ref, sl_ref, o_ref, lse_ref, *,
                     heads, length, dilation, radius, qblk):
    scale = HEAD_DIM ** -0.5
    for h in range(heads):
        c0 = h * HEAD_DIM
        qn = _head_norm(q_ref[:, c0:c0 + HEAD_DIM], qg_ref[...])
        kn = _head_norm(k_ref[:, c0:c0 + HEAD_DIM], kg_ref[...])
        v = v_ref[:, c0:c0 + HEAD_DIM]
        slope = sl_ref[:, c0:c0 + 1]
        for s0 in range(0, length, qblk):
            ks = max(0, s0 - radius)
            ke = min(length, s0 + qblk + radius)
            nk = ke - ks
            scores = lax.dot_general(qn[s0:s0 + qblk], kn[ks:ke], (((1,), (1,)), ((), ())),
                                     preferred_element_type=F32) * scale
            jq = s0 + lax.broadcasted_iota(jnp.int32, (qblk, nk), 0)
            jk = ks + lax.broadcasted_iota(jnp.int32, (qblk, nk), 1)
            dist = jnp.abs(jq - jk)
            bias = -slope * (dist * dilation).astype(F32)
            logits = jnp.where(dist <= radius, scores + bias, NEG_INF)
            mx = jnp.max(logits, axis=-1, keepdims=True)
            p = jnp.exp(logits - mx)
            den = jnp.sum(p, axis=-1, keepdims=True)
            out = jnp.dot((p / den).astype(BF16), v[ks:ke], preferred_element_type=F32)
            o_ref[s0:s0 + qblk, c0:c0 + HEAD_DIM] = out.astype(o_ref.dtype)
            lse_ref[s0:s0 + qblk, c0:c0 + HEAD_DIM] = jnp.broadcast_to(mx + jnp.log(den), (qblk, HEAD_DIM))


def _dilated_group(z, group, q_gain, k_gain, slopes, n_seq, seq):
    window, dilation = C_PAIRS[group]
    radius = window // (2 * dilation)
    assert seq % (dilation * radius) == 0
    length = seq // dilation
    zw = z.shape[1]
    heads = {1: 1, 4: 4}.get(dilation, C_HEADS)
    bw = heads * HEAD_DIM
    qblk = min(length, 256)
    zr = z.reshape(n_seq, length, dilation * zw)
    base = group * 3 * C_WIDTH

    def in_map(part):
        def f(b, r, hb):
            return (b, 0, (r * zw + base + part * C_WIDTH) // bw + hb)
        return f

    out_map = lambda b, r, hb: (b, 0, r * (C_WIDTH // bw) + hb)
    gain_spec = pl.BlockSpec((1, HEAD_DIM), lambda b, r, hb: (0, 0))
    out, lse = pl.pallas_call(
        functools.partial(_dil_attn_kernel, heads=heads, length=length, dilation=dilation,
                          radius=radius, qblk=qblk),
        grid=(n_seq, dilation, C_HEADS // heads),
        in_specs=[pl.BlockSpec((None, length, bw), in_map(0)),
                  pl.BlockSpec((None, length, bw), in_map(1)),
                  pl.BlockSpec((None, length, bw), in_map(2)),
                  gain_spec, gain_spec,
                  pl.BlockSpec((1, bw), lambda b, r, hb: (0, hb))],
        out_specs=[pl.BlockSpec((None, length, bw), out_map),
                   pl.BlockSpec((None, length, bw), out_map)],
        out_shape=[jax.ShapeDtypeStruct((n_seq, length, dilation * C_WIDTH), BF16),
                   jax.ShapeDtypeStruct((n_seq, length, dilation * C_WIDTH), F32)],
        compiler_params=_params("parallel", "parallel", "parallel"),
        name=f"dilated_attention_g{group}",
    )(zr, zr, zr, q_gain.reshape(1, HEAD_DIM).astype(F32), k_gain.reshape(1, HEAD_DIM).astype(F32), slopes)
    return out.reshape(n_seq * seq, C_WIDTH), lse.reshape(n_seq * seq, C_WIDTH)


def _merge_kernel(o0_ref, o1_ref, o2_ref, l0_ref, l1_ref, l2_ref, y_ref):
    l0, l1, l2 = l0_ref[...], l1_ref[...], l2_ref[...]
    mx = jnp.maximum(jnp.maximum(l0, l1), l2)
    e0, e1, e2 = jnp.exp(l0 - mx), jnp.exp(l1 - mx), jnp.exp(l2 - mx)
    den = e0 + e1 + e2
    y = ((e0 / den) * o0_ref[...].astype(F32) + (e1 / den) * o1_ref[...].astype(F32)
         + (e2 / den) * o2_ref[...].astype(F32))
    y_ref[...] = y.astype(y_ref.dtype)


def dilated_mixture(z, q_gain, k_gain, n_seq, seq):
    slopes = 2.0 ** (-ALIBI_MAX * jnp.arange(1, C_HEADS + 1, dtype=F32) / C_HEADS)
    slopes = jnp.repeat(slopes, HEAD_DIM).reshape(1, C_WIDTH)
    outs, lses = [], []
    for g in range(len(C_PAIRS)):
        o, l = _dilated_group(z, g, q_gain[g], k_gain[g], slopes, n_seq, seq)
        outs.append(o)
        lses.append(l)
    m = z.shape[0]
    bm = _pick(m, (512, 256))
    spec = pl.BlockSpec((bm, C_WIDTH), lambda i: (i, 0))
    return pl.pallas_call(
        _merge_kernel,
        grid=(m // bm,),
        in_specs=[spec] * 6,
        out_specs=spec,
        out_shape=jax.ShapeDtypeStruct((m, C_WIDTH), BF16),
        compiler_params=_params("parallel"),
        name="dilated_merge",
    )(*outs, *lses)


def _mem_attn_kernel(q_ref, kv_ref, qg_ref, kg_ref, o_ref):
    scale = HEAD_DIM ** -0.5
    for h in range(MEM_HEADS):
        c0 = h * HEAD_DIM
        qn = _head_norm(q_ref[:, c0:c0 + HEAD_DIM], qg_ref[...])
        kn = _head_norm(kv_ref[:, c0:c0 + HEAD_DIM], kg_ref[...])
        v = kv_ref[:, MEM_WIDTH + c0:MEM_WIDTH + c0 + HEAD_DIM].astype(BF16)
        s = lax.dot_general(qn, kn, (((1,), (1,)), ((), ())), preferred_element_type=F32) * scale
        mx = jnp.max(s, axis=-1, keepdims=True)
        e = jnp.exp(s - mx)
        p = e / jnp.sum(e, axis=-1, keepdims=True)
        o_ref[:, c0:c0 + HEAD_DIM] = jnp.dot(p.astype(BF16), v, preferred_element_type=F32).astype(o_ref.dtype)


def memory_attention(q, kv, q_gain, k_gain, n_seq, seq, n_mem):
    bm = _pick(seq, (1024, 512, 256))
    nb = seq // bm
    gain_spec = pl.BlockSpec((1, HEAD_DIM), lambda b, i: (0, 0))
    return pl.pallas_call(
        _mem_attn_kernel,
        grid=(n_seq, nb),
        in_specs=[pl.BlockSpec((bm, MEM_WIDTH), lambda b, i: (b * nb + i, 0)),
                  pl.BlockSpec((n_mem, 2 * MEM_WIDTH), lambda b, i: (b, 0)),
                  gain_spec, gain_spec],
        out_specs=pl.BlockSpec((bm, MEM_WIDTH), lambda b, i: (b * nb + i, 0)),
        out_shape=jax.ShapeDtypeStruct((n_seq * seq, MEM_WIDTH), BF16),
        compiler_params=_params("parallel", "arbitrary"),
        name="memory_attention",
    )(q, kv, q_gain.reshape(1, HEAD_DIM).astype(F32), k_gain.reshape(1, HEAD_DIM).astype(F32))


def _trunk(x, mem, p, depth):
    n_seq, seq, d = x.shape
    n_mem = mem.shape[1]
    x = x.reshape(n_seq * seq, d)
    mem = mem.reshape(n_seq * n_mem, d)
    for i in range(depth):
        kind, j = i % N_MIXERS, i // N_MIXERS
        h = rmsnorm(x, p["mixer_norm"][i])
        mem_n = rmsnorm(mem, p["mem_norm"][i])
        w_in, w_out = p["w_in"][i], p["w_out"][i]
        y_width = w_in.shape[1] - MEM_WIDTH
        z = matmul([h], [w_in[:, :y_width]], out_dtype=BF16, name="in_proj")
        q_mem = matmul([h], [w_in[:, y_width:]], out_dtype=BF16, name="in_proj_mem")
        if kind == 0:
            y = gated_chunk_mlp(z, p["a_w_s"][j], p["a_b_s"][j], p["a_v_gain"][j])
        elif kind == 1:
            y = fourier_mix(z, n_seq, seq)
        else:
            y = dilated_mixture(z, p["c_q_gain"][j], p["c_k_gain"][j], n_seq, seq)
        kv = matmul([mem_n], [p["mem_w_kv"][i]], out_dtype=F32, name="mem_kv")
        mo = memory_attention(q_mem, kv, p["mem_q_gain"][i], p["mem_k_gain"][i], n_seq, seq, n_mem)
        yw = y.shape[1]
        x = matmul([y, mo], [w_out[:yw], w_out[yw:]], out_dtype=F32, res=x, name="out_proj")
        h = rmsnorm(x, p["ffn_norm"][i])
        f = matmul([h], [p["w_ff1"][i]], out_dtype=BF16, relu2=True, name="ffn_up")
        x = matmul_kred_res(f, p["w_ff2"][i], x, name="ffn_down")
    return x.reshape(n_seq, seq, d)


@jax.jit
def kernel(x_prompt, x_sample, mem_prompt, mem_sample, mixer_norm, mem_norm, ffn_norm, mem_w_kv,
           mem_q_gain, mem_k_gain, w_ff1, w_ff2, a_w_in, a_w_out, a_w_s, a_b_s, a_v_gain,
           b_w_in, b_w_out, c_w_in, c_w_out, c_q_gain, c_k_gain):
    depth = mixer_norm.shape[0]
    mixer_w_in = (a_w_in, b_w_in, c_w_in)
    mixer_w_out = (a_w_out, b_w_out, c_w_out)
    p = {
        "mixer_norm": mixer_norm, "mem_norm": mem_norm, "ffn_norm": ffn_norm,
        "mem_q_gain": mem_q_gain, "mem_k_gain": mem_k_gain,
        "a_w_s": a_w_s, "a_b_s": a_b_s, "a_v_gain": a_v_gain,
        "c_q_gain": c_q_gain, "c_k_gain": c_k_gain,
        "mem_w_kv": [mem_w_kv[i].astype(BF16) for i in range(depth)],
        "w_ff1": [w_ff1[i].astype(BF16) for i in range(depth)],
        "w_ff2": [w_ff2[i].astype(BF16) for i in range(depth)],
        "w_in": [mixer_w_in[i % N_MIXERS][i // N_MIXERS].astype(BF16) for i in range(depth)],
        "w_out": [mixer_w_out[i % N_MIXERS][i // N_MIXERS].astype(BF16) for i in range(depth)],
    }
    return _trunk(x_prompt, mem_prompt, p, depth), _trunk(x_sample, mem_sample, p, depth)
```

```python
import functools
import math

import jax
import jax.numpy as jnp
from jax import lax
from jax.experimental import pallas as pl
from jax.experimental.pallas import tpu as pltpu

F32 = jnp.float32
BF16 = jnp.bfloat16

EPS = 1e-6
NEG_INF = -1e30
HEAD_DIM = 128
MEM_HEADS = 4
MEM_WIDTH = MEM_HEADS * HEAD_DIM
A_CHUNK = 128
A_GROUPS = 8
B_GROUPS = 4
C_PAIRS = ((128, 1), (512, 4), (2048, 16))
C_GROUPS = len(C_PAIRS)
C_HEADS = 8
C_WIDTH = C_HEADS * HEAD_DIM
ALIBI_MAX = 8.0
N_MIXERS = 3

V7X_VMEM_BYTES = 64 * 1024 * 1024
VMEM_LIMIT_BYTES = V7X_VMEM_BYTES - 6 * 1024 * 1024
V7X_LANES = 128


def _params(*semantics):
    return pltpu.CompilerParams(dimension_semantics=semantics, vmem_limit_bytes=VMEM_LIMIT_BYTES)


def _pick(n, candidates):
    for c in candidates:
        if n % c == 0:
            return c
    raise ValueError(f"no block size in {candidates} divides {n}")


def _gelu(x):
    return 0.5 * x * (1.0 + lax.erf(x * (1.0 / math.sqrt(2.0))))


def _row_inv_rms(a_ref):
    rows, k = a_ref.shape
    part = jnp.zeros((rows, V7X_LANES), F32)
    for c in range(k // V7X_LANES):
        blk = a_ref[:, c * V7X_LANES:(c + 1) * V7X_LANES].astype(F32)
        part = part + blk * blk
    return lax.rsqrt(jnp.sum(part, axis=-1, keepdims=True) / k + EPS)


def _mm_kernel(*refs, n_ops, norm, act, has_res, n_out):
    a_refs = refs[:n_ops]
    w_refs = refs[n_ops:2 * n_ops]
    pos = 2 * n_ops
    res_ref = refs[pos] if has_res else None
    pos += has_res
    o_refs = refs[pos:pos + n_out]
    if norm:
        inv_ref = refs[pos + n_out]

        @pl.when(pl.program_id(1) == 0)
        def _():
            inv_ref[...] = jnp.broadcast_to(_row_inv_rms(a_refs[0]), inv_ref.shape)

    acc = None
    for a_ref, w_ref in zip(a_refs, w_refs):
        d = jnp.dot(a_ref[...], w_ref[...], preferred_element_type=F32)
        acc = d if acc is None else acc + d
    if norm:
        acc = acc * inv_ref[:, 0:1]
    if act == "relu2":
        r = jnp.maximum(acc, 0.0)
        acc = r * r
    elif act == "gelu":
        acc = _gelu(acc)
    if has_res:
        acc = res_ref[...] + acc
    for o_ref in o_refs:
        o_ref[...] = acc.astype(o_ref.dtype)


def _mm_blocks(m, n, k_total, out_bytes, has_res):
    for bm, bn in ((1024, 1024), (1024, 512), (512, 1024), (512, 512), (256, 512)):
        if m % bm or n % bn:
            continue
        windows = 2 * (bm * k_total * 2 + k_total * bn * 2 + bm * bn * (out_bytes + 4 * has_res))
        if windows + 3 * bm * bn * 4 <= VMEM_LIMIT_BYTES:
            return bm, bn
    raise ValueError(f"no matmul blocks fit m={m} n={n} k={k_total}")


def matmul(a_list, w_list, *, out_dtypes, n_cols=None, col0=0, row0s=None, norm=False, act=None,
           res=None, name="matmul"):
    m = a_list[0].shape[0]
    n = n_cols if n_cols is not None else w_list[0].shape[1]
    row0s = row0s or [0] * len(a_list)
    ks = [a.shape[1] for a in a_list]
    out_bytes = sum(jnp.dtype(d).itemsize for d in out_dtypes)
    bm, bn = _mm_blocks(m, n, sum(ks), out_bytes, res is not None)
    assert col0 % bn == 0 and all(r % k == 0 for r, k in zip(row0s, ks))
    cb = col0 // bn
    in_specs = [pl.BlockSpec((bm, k), lambda i, j: (i, 0)) for k in ks]
    in_specs += [pl.BlockSpec((k, bn), functools.partial(lambda i, j, rb: (rb, cb + j), rb=r // k))
                 for r, k in zip(row0s, ks)]
    args = list(a_list) + list(w_list)
    if res is not None:
        in_specs.append(pl.BlockSpec((bm, bn), lambda i, j: (i, j)))
        args.append(res)
    out_spec = pl.BlockSpec((bm, bn), lambda i, j: (i, j))
    outs = pl.pallas_call(
        functools.partial(_mm_kernel, n_ops=len(a_list), norm=norm, act=act, has_res=res is not None,
                          n_out=len(out_dtypes)),
        grid=(m // bm, n // bn),
        in_specs=in_specs,
        out_specs=[out_spec] * len(out_dtypes),
        out_shape=[jax.ShapeDtypeStruct((m, n), d) for d in out_dtypes],
        scratch_shapes=[pltpu.VMEM((bm, V7X_LANES), F32)] if norm else [],
        compiler_params=_params("parallel", "arbitrary"),
        name=name,
    )(*args)
    return outs if len(outs) > 1 else outs[0]


def _mm_kred_kernel(a_ref, w_ref, res_ref, o_ref, *maybe_bf16_ref):
    kk = pl.program_id(2)

    @pl.when(kk == 0)
    def _():
        o_ref[...] = res_ref[...]

    o_ref[...] += jnp.dot(a_ref[...], w_ref[...], preferred_element_type=F32)

    for ob_ref in maybe_bf16_ref:
        @pl.when(kk == pl.num_programs(2) - 1)
        def _():
            ob_ref[...] = o_ref[...].astype(ob_ref.dtype)


def matmul_kred_res(a, w, res, *, with_bf16, name="matmul_kred"):
    m, k = a.shape
    n = w.shape[1]
    bm = _pick(m, (1024, 512, 256))
    bn = _pick(n, (1024, 512))
    bk = _pick(k, (2048, 1024))
    out_spec = pl.BlockSpec((bm, bn), lambda i, j, kk: (i, j))
    out_shape = [jax.ShapeDtypeStruct((m, n), F32)]
    if with_bf16:
        out_shape.append(jax.ShapeDtypeStruct((m, n), BF16))
    outs = pl.pallas_call(
        _mm_kred_kernel,
        grid=(m // bm, n // bn, k // bk),
        in_specs=[pl.BlockSpec((bm, bk), lambda i, j, kk: (i, kk)),
                  pl.BlockSpec((bk, bn), lambda i, j, kk: (kk, j)),
                  pl.BlockSpec((bm, bn), lambda i, j, kk: (i, j))],
        out_specs=[out_spec] * len(out_shape),
        out_shape=out_shape,
        compiler_params=_params("parallel", "parallel", "arbitrary"),
        name=name,
    )(a, w, res)
    return outs if with_bf16 else outs[0]


def _gate_kernel(u_ref, v_ref, vg_ref, ws_ref, bs_ref, o_ref, *, n_chunks, gw):
    inv = _row_inv_rms(v_ref)

    def gate(g, carry):
        c0 = pl.multiple_of(g * gw, gw)
        vn = ((v_ref[:, pl.ds(c0, gw)].astype(F32) * inv) * vg_ref[:, pl.ds(c0, gw)]).astype(BF16)
        u = u_ref[:, pl.ds(c0, gw)].astype(F32)
        w = ws_ref[g]
        b = bs_ref[g]
        for c in range(n_chunks):
            r0 = c * A_CHUNK
            vm = jnp.dot(w, vn[r0:r0 + A_CHUNK], preferred_element_type=F32) + b
            o_ref[r0:r0 + A_CHUNK, pl.ds(c0, gw)] = (u[r0:r0 + A_CHUNK] * vm).astype(o_ref.dtype)
        return carry

    lax.fori_loop(0, A_GROUPS, gate, 0)


def gated_chunk_mlp(uv, w_s, b_s, v_gain):
    m, w2 = uv.shape
    width = w2 // 2
    gw = width // A_GROUPS
    rows = _pick(m, (512, 256, 128))
    return pl.pallas_call(
        functools.partial(_gate_kernel, n_chunks=rows // A_CHUNK, gw=gw),
        grid=(m // rows,),
        in_specs=[pl.BlockSpec((rows, width), lambda i: (i, 0)),
                  pl.BlockSpec((rows, width), lambda i: (i, 1)),
                  pl.BlockSpec((1, width), lambda i: (0, 0)),
                  pl.BlockSpec((A_GROUPS, A_CHUNK, A_CHUNK), lambda i: (0, 0, 0)),
                  pl.BlockSpec((A_GROUPS, A_CHUNK, 1), lambda i: (0, 0, 0))],
        out_specs=pl.BlockSpec((rows, width), lambda i: (i, 0)),
        out_shape=jax.ShapeDtypeStruct((m, width), BF16),
        compiler_params=_params("parallel"),
        name="gated_chunk_mlp",
    )(uv, uv, v_gain.reshape(1, width).astype(F32), w_s.astype(BF16),
      b_s.reshape(A_GROUPS, A_CHUNK, 1).astype(F32))


def _dft_tables(n):
    idx = lax.iota(jnp.int32, n)
    prod = (idx[:, None] * idx[None, :]) % n
    theta = prod.astype(F32) * (2.0 * math.pi / n)
    scale = 1.0 / math.sqrt(n)
    return jnp.cos(theta) * scale, -jnp.sin(theta) * scale


def _chan_dft_kernel(z_ref, w_ref, o_ref):
    o_ref[...] = jnp.dot(z_ref[...], w_ref[...], preferred_element_type=F32).astype(o_ref.dtype)


def _pos_dft_kernel(c_ref, s_ref, pc_ref, ps_ref, o_ref):
    acc = jnp.dot(c_ref[...], pc_ref[...], preferred_element_type=F32)
    acc = acc + jnp.dot(s_ref[...], ps_ref[...], preferred_element_type=F32)
    o_ref[...] = acc.astype(o_ref.dtype)


def fourier_mix(z, n_seq, seq):
    m, width = z.shape
    gc = width // B_GROUPS
    cc, sc = _dft_tables(gc)
    w_ch = jnp.concatenate([cc, -sc], axis=1).astype(BF16)
    cp, sp = _dft_tables(seq)
    cp = cp.astype(BF16)
    nsp = sp.astype(BF16)
    bm = _pick(m, (1024, 512, 256))
    p = pl.pallas_call(
        _chan_dft_kernel,
        grid=(m // bm, B_GROUPS),
        in_specs=[pl.BlockSpec((bm, gc), lambda i, g: (i, g)),
                  pl.BlockSpec((gc, 2 * gc), lambda i, g: (0, 0))],
        out_specs=pl.BlockSpec((bm, 2 * gc), lambda i, g: (i, g)),
        out_shape=jax.ShapeDtypeStruct((m, 2 * width), BF16),
        compiler_params=_params("parallel", "parallel"),
        name="fourier_channels",
    )(z, w_ch)
    bo = _pick(seq, (1024, 512, 256))
    nb = seq // bo
    return pl.pallas_call(
        _pos_dft_kernel,
        grid=(n_seq, B_GROUPS, nb),
        in_specs=[pl.BlockSpec((bo, seq), lambda b, g, i: (i, 0)),
                  pl.BlockSpec((bo, seq), lambda b, g, i: (i, 0)),
                  pl.BlockSpec((seq, gc), lambda b, g, i: (b, 2 * g)),
                  pl.BlockSpec((seq, gc), lambda b, g, i: (b, 2 * g + 1))],
        out_specs=pl.BlockSpec((bo, gc), lambda b, g, i: (b * nb + i, g)),
        out_shape=jax.ShapeDtypeStruct((m, width), BF16),
        compiler_params=_params("parallel", "parallel", "arbitrary"),
        name="fourier_positions",
    )(cp, nsp, p, p)


def _head_norm(x, g):
    x = x.astype(F32)
    ms = jnp.mean(x * x, axis=-1, keepdims=True)
    return (x * lax.rsqrt(ms + EPS)) * g


def _band_bias(n_blocks, qblk, radius, dilation, slope, length):
    kw = qblk + 2 * radius
    row = lax.broadcasted_iota(jnp.int32, (qblk, kw), 0)
    col = lax.broadcasted_iota(jnp.int32, (qblk, kw), 1)
    dist = jnp.abs(row - (col - radius))
    inner = jnp.where(dist <= radius, -slope * (dist * dilation).astype(F32), NEG_INF)
    first = jnp.where(col < radius, NEG_INF, inner)
    last_col = length - (n_blocks - 1) * qblk + radius
    last = jnp.where(col >= last_col, NEG_INF, inner)
    if n_blocks == 1:
        return jnp.where(col >= last_col, NEG_INF, first)[None]
    return jnp.stack([first] + [inner] * (n_blocks - 2) + [last])


def _windows(x, n_blocks, qblk, radius):
    pad = jnp.zeros((radius, x.shape[1]), x.dtype)
    xp = jnp.concatenate([pad, x, pad], axis=0)
    return jnp.stack([xp[t * qblk:t * qblk + qblk + 2 * radius] for t in range(n_blocks)])


def _block_softmax_pv(q, k, v, bias):
    scores = jnp.einsum("bqc,bkc->bqk", q, k, preferred_element_type=F32) * (HEAD_DIM ** -0.5)
    logits = scores + bias
    mx = jnp.max(logits, axis=-1, keepdims=True)
    p = jnp.exp(logits - mx)
    den = jnp.sum(p, axis=-1, keepdims=True)
    pv = jnp.einsum("bqk,bkc->bqc", p.astype(BF16), v, preferred_element_type=F32)
    return mx, den, pv


def _dil_attn_kernel(*refs, seq):
    qkv_refs = refs[:3 * C_GROUPS]
    qg_ref, kg_ref, sl_ref, o_ref, q_s, k_s, v_s, m_s, l_s, acc_s = refs[3 * C_GROUPS:]
    slope = sl_ref[:, 0:1]
    lanes = (HEAD_DIM,)
    for g, (window, dilation) in enumerate(C_PAIRS):
        radius = window // (2 * dilation)
        length = seq // dilation
        qblk = min(length, 256)
        nb = length // qblk
        q_ref, k_ref, v_ref = qkv_refs[3 * g:3 * g + 3]
        q_s[...] = _head_norm(q_ref[...], qg_ref[g:g + 1, :])
        k_s[...] = _head_norm(k_ref[...], kg_ref[g:g + 1, :])
        v_s[...] = v_ref[...].astype(F32)

        def residue(ref, r):
            return ref[pl.ds(r, length, stride=dilation), :] if dilation > 1 else ref[...]

        qb = jnp.concatenate([residue(q_s, r).astype(BF16).reshape(nb, qblk, HEAD_DIM) for r in range(dilation)])
        kb = jnp.concatenate([_windows(residue(k_s, r).astype(BF16), nb, qblk, radius) for r in range(dilation)])
        vb = jnp.concatenate([_windows(residue(v_s, r).astype(BF16), nb, qblk, radius) for r in range(dilation)])
        bias = _band_bias(nb, qblk, radius, dilation, slope, length)
        bias = jnp.concatenate([bias] * dilation)
        mx, den, pv = _block_softmax_pv(qb, kb, vb, bias)
        mx = jnp.broadcast_to(mx, mx.shape[:2] + lanes)
        den = jnp.broadcast_to(den, den.shape[:2] + lanes)
        if g == 0:
            m_s[...] = mx.reshape(seq, HEAD_DIM)
            l_s[...] = den.reshape(seq, HEAD_DIM)
            acc_s[...] = pv.reshape(seq, HEAD_DIM)
            continue
        for r in range(dilation):
            rows = slice(r * nb, (r + 1) * nb)
            m_old = residue(m_s, r)
            m_blk = mx[rows].reshape(length, HEAD_DIM)
            m_new = jnp.maximum(m_old, m_blk)
            a = jnp.exp(m_old - m_new)
            b = jnp.exp(m_blk - m_new)
            l_new = a * residue(l_s, r) + b * den[rows].reshape(length, HEAD_DIM)
            acc_new = a * residue(acc_s, r) + b * pv[rows].reshape(length, HEAD_DIM)
            m_s[pl.ds(r, length, stride=dilation), :] = m_new
            l_s[pl.ds(r, length, stride=dilation), :] = l_new
            acc_s[pl.ds(r, length, stride=dilation), :] = acc_new
    o_ref[...] = (acc_s[...] / l_s[...]).astype(o_ref.dtype)


def dilated_mixture(z, q_gain, k_gain, n_seq, seq):
    for window, dilation in C_PAIRS:
        assert seq % (dilation * (window // (2 * dilation))) == 0
    slopes = 2.0 ** (-ALIBI_MAX * jnp.arange(1, C_HEADS + 1, dtype=F32) / C_HEADS)
    slopes = jnp.repeat(slopes, HEAD_DIM).reshape(1, C_WIDTH)
    heads_per_part = C_WIDTH // HEAD_DIM

    def qkv_spec(g, part):
        col = (g * 3 + part) * heads_per_part
        return pl.BlockSpec((seq, HEAD_DIM), lambda b, h: (b, col + h))

    gain_spec = pl.BlockSpec((C_GROUPS, HEAD_DIM), lambda b, h: (0, 0))
    state = pltpu.VMEM((seq, HEAD_DIM), F32)
    return pl.pallas_call(
        functools.partial(_dil_attn_kernel, seq=seq),
        grid=(n_seq, C_HEADS),
        in_specs=[qkv_spec(g, part) for g in range(C_GROUPS) for part in range(3)]
                 + [gain_spec, gain_spec, pl.BlockSpec((1, HEAD_DIM), lambda b, h: (0, h))],
        out_specs=pl.BlockSpec((seq, HEAD_DIM), lambda b, h: (b, h)),
        out_shape=jax.ShapeDtypeStruct((n_seq * seq, C_WIDTH), BF16),
        scratch_shapes=[state] * 6,
        compiler_params=_params("parallel", "parallel"),
        name="dilated_attention",
    )(*([z] * (3 * C_GROUPS)), q_gain.astype(F32), k_gain.astype(F32), slopes)


def _mem_attn_kernel(q_ref, kv_ref, qg_ref, kg_ref, o_ref):
    scale = HEAD_DIM ** -0.5
    for h in range(MEM_HEADS):
        c0 = h * HEAD_DIM
        qn = _head_norm(q_ref[:, c0:c0 + HEAD_DIM], qg_ref[...]).astype(BF16)
        kn = _head_norm(kv_ref[:, c0:c0 + HEAD_DIM], kg_ref[...]).astype(BF16)
        v = kv_ref[:, MEM_WIDTH + c0:MEM_WIDTH + c0 + HEAD_DIM].astype(BF16)
        s = lax.dot_general(qn, kn, (((1,), (1,)), ((), ())), preferred_element_type=F32) * scale
        mx = jnp.max(s, axis=-1, keepdims=True)
        e = jnp.exp(s - mx)
        p = e / jnp.sum(e, axis=-1, keepdims=True)
        o_ref[:, c0:c0 + HEAD_DIM] = jnp.dot(p.astype(BF16), v, preferred_element_type=F32).astype(o_ref.dtype)


def memory_attention(q, kv, q_gain, k_gain, n_seq, seq, n_mem):
    bm = _pick(seq, (1024, 512, 256))
    nb = seq // bm
    gain_spec = pl.BlockSpec((1, HEAD_DIM), lambda b, i: (0, 0))
    return pl.pallas_call(
        _mem_attn_kernel,
        grid=(n_seq, nb),
        in_specs=[pl.BlockSpec((bm, MEM_WIDTH), lambda b, i: (b * nb + i, 0)),
                  pl.BlockSpec((n_mem, 2 * MEM_WIDTH), lambda b, i: (b, 0)),
                  gain_spec, gain_spec],
        out_specs=pl.BlockSpec((bm, MEM_WIDTH), lambda b, i: (b * nb + i, 0)),
        out_shape=jax.ShapeDtypeStruct((n_seq * seq, MEM_WIDTH), BF16),
        compiler_params=_params("parallel", "arbitrary"),
        name="memory_attention",
    )(q, kv, q_gain.reshape(1, HEAD_DIM).astype(F32), k_gain.reshape(1, HEAD_DIM).astype(F32))


def _trunk(x, mem, p, depth):
    n_seq, seq, d = x.shape
    n_mem = mem.shape[1]
    x = x.reshape(n_seq * seq, d)
    xb = x.astype(BF16)
    memb = mem.reshape(n_seq * n_mem, d).astype(BF16)
    for i in range(depth):
        kind, j = i % N_MIXERS, i // N_MIXERS
        w_in, w_out = p["w_in"][i], p["w_out"][i]
        y_width = w_in.shape[1] - MEM_WIDTH
        z = matmul([xb], [w_in], out_dtypes=[BF16], n_cols=y_width, norm=True,
                   act="gelu" if kind == 0 else None, name="in_proj")
        q_mem = matmul([xb], [w_in], out_dtypes=[BF16], n_cols=MEM_WIDTH, col0=y_width, norm=True,
                       name="in_proj_mem")
        if kind == 0:
            y = gated_chunk_mlp(z, p["a_w_s"][j], p["a_b_s"][j], p["a_v_gain"][j])
        elif kind == 1:
            y = fourier_mix(z, n_seq, seq)
        else:
            y = dilated_mixture(z, p["c_q_gain"][j], p["c_k_gain"][j], n_seq, seq)
        kv = matmul([memb], [p["mem_w_kv"][i]], out_dtypes=[F32], norm=True, name="mem_kv")
        mo = memory_attention(q_mem, kv, p["mem_q_gain"][i], p["mem_k_gain"][i], n_seq, seq, n_mem)
        x, xb = matmul([y, mo], [w_out, w_out], out_dtypes=[F32, BF16], row0s=[0, y.shape[1]], res=x,
                       name="out_proj")
        f = matmul([xb], [p["w_ff1"][i]], out_dtypes=[BF16], norm=True, act="relu2", name="ffn_up")
        if i + 1 < depth:
            x, xb = matmul_kred_res(f, p["w_ff2"][i], x, with_bf16=True, name="ffn_down")
        else:
            x = matmul_kred_res(f, p["w_ff2"][i], x, with_bf16=False, name="ffn_down")
    return x.reshape(n_seq, seq, d)


def _fold_gain(w, g):
    return (g.astype(F32)[:, None] * w).astype(BF16)


@jax.jit
def kernel(x_prompt, x_sample, mem_prompt, mem_sample, mixer_norm, mem_norm, ffn_norm, mem_w_kv,
           mem_q_gain, mem_k_gain, w_ff1, w_ff2, a_w_in, a_w_out, a_w_s, a_b_s, a_v_gain,
           b_w_in, b_w_out, c_w_in, c_w_out, c_q_gain, c_k_gain):
    depth = mixer_norm.shape[0]
    mixer_w_in = (a_w_in, b_w_in, c_w_in)
    mixer_w_out = (a_w_out, b_w_out, c_w_out)
    p = {
        "mem_q_gain": mem_q_gain, "mem_k_gain": mem_k_gain,
        "a_w_s": a_w_s, "a_b_s": a_b_s, "a_v_gain": a_v_gain,
        "c_q_gain": c_q_gain, "c_k_gain": c_k_gain,
        "mem_w_kv": [_fold_gain(mem_w_kv[i], mem_norm[i]) for i in range(depth)],
        "w_ff1": [_fold_gain(w_ff1[i], ffn_norm[i]) for i in range(depth)],
        "w_ff2": [w_ff2[i].astype(BF16) for i in range(depth)],
        "w_in": [_fold_gain(mixer_w_in[i % N_MIXERS][i // N_MIXERS], mixer_norm[i]) for i in range(depth)],
        "w_out": [mixer_w_out[i % N_MIXERS][i // N_MIXERS].astype(BF16) for i in range(depth)],
    }
    return _trunk(x_prompt, mem_prompt, p, depth), _trunk(x_sample, mem_sample, p, depth)
```

```python
import functools
import math

import jax
import jax.numpy as jnp
from jax import lax
from jax.experimental import pallas as pl
from jax.experimental.pallas import tpu as pltpu

F32 = jnp.float32
BF16 = jnp.bfloat16

EPS = 1e-6
NEG_INF = -1e30
HEAD_DIM = 128
MEM_HEADS = 4
MEM_WIDTH = MEM_HEADS * HEAD_DIM
A_CHUNK = 128
A_GROUPS = 8
B_GROUPS = 4
C_PAIRS = ((128, 1), (512, 4), (2048, 16))
C_GROUPS = len(C_PAIRS)
C_HEADS = 8
C_WIDTH = C_HEADS * HEAD_DIM
ALIBI_MAX = 8.0
N_MIXERS = 3

V7X_VMEM_BYTES = 64 * 1024 * 1024
VMEM_LIMIT_BYTES = V7X_VMEM_BYTES - 2 * 1024 * 1024
V7X_LANES = 128


def _params(*semantics):
    return pltpu.CompilerParams(dimension_semantics=semantics, vmem_limit_bytes=VMEM_LIMIT_BYTES)


def _pick(n, candidates):
    for c in candidates:
        if n % c == 0:
            return c
    raise ValueError(f"no block size in {candidates} divides {n}")


def _gelu(x):
    return 0.5 * x * (1.0 + lax.erf(x * (1.0 / math.sqrt(2.0))))


def _row_inv_rms(a_ref):
    rows, k = a_ref.shape
    part = jnp.zeros((rows, V7X_LANES), F32)
    for c in range(k // V7X_LANES):
        blk = a_ref[:, c * V7X_LANES:(c + 1) * V7X_LANES].astype(F32)
        part = part + blk * blk
    return lax.rsqrt(jnp.sum(part, axis=-1, keepdims=True) / k + EPS)


def _mm_kernel(*refs, n_ops, norm, act, has_res, n_out):
    a_refs = refs[:n_ops]
    w_refs = refs[n_ops:2 * n_ops]
    pos = 2 * n_ops
    res_ref = refs[pos] if has_res else None
    pos += has_res
    o_refs = refs[pos:pos + n_out]
    if norm:
        inv_ref = refs[pos + n_out]

        @pl.when(pl.program_id(1) == 0)
        def _():
            inv_ref[...] = jnp.broadcast_to(_row_inv_rms(a_refs[0]), inv_ref.shape)

    acc = None
    for a_ref, w_ref in zip(a_refs, w_refs):
        d = jnp.dot(a_ref[...], w_ref[...], preferred_element_type=F32)
        acc = d if acc is None else acc + d
    if norm:
        acc = acc * inv_ref[:, 0:1]
    if act == "relu2":
        r = jnp.maximum(acc, 0.0)
        acc = r * r
    elif act == "gelu":
        acc = _gelu(acc)
    if has_res:
        acc = res_ref[...] + acc
    for o_ref in o_refs:
        o_ref[...] = acc.astype(o_ref.dtype)


def _mm_blocks(m, n, k_total, out_bytes, has_res):
    for bm, bn in ((1024, 1024), (1024, 512), (512, 1024), (512, 512), (256, 512)):
        if m % bm or n % bn:
            continue
        windows = 2 * (bm * k_total * 2 + k_total * bn * 2 + bm * bn * (out_bytes + 4 * has_res))
        if windows + 3 * bm * bn * 4 <= VMEM_LIMIT_BYTES:
            return bm, bn
    raise ValueError(f"no matmul blocks fit m={m} n={n} k={k_total}")


def matmul(a_list, w_list, *, out_dtypes, layer=0, n_cols=None, col0=0, row0s=None, norm=False, act=None,
           res=None, name="matmul"):
    m = a_list[0].shape[0]
    n = n_cols if n_cols is not None else w_list[0].shape[2]
    row0s = row0s or [0] * len(a_list)
    ks = [a.shape[1] for a in a_list]
    out_bytes = sum(jnp.dtype(d).itemsize for d in out_dtypes)
    bm, bn = _mm_blocks(m, n, sum(ks), out_bytes, res is not None)
    assert col0 % bn == 0 and all(r % k == 0 for r, k in zip(row0s, ks))
    cb = col0 // bn
    in_specs = [pl.BlockSpec((bm, k), lambda i, j: (i, 0)) for k in ks]
    in_specs += [pl.BlockSpec((None, k, bn), functools.partial(lambda i, j, rb: (layer, rb, cb + j), rb=r // k))
                 for r, k in zip(row0s, ks)]
    args = list(a_list) + list(w_list)
    if res is not None:
        in_specs.append(pl.BlockSpec((bm, bn), lambda i, j: (i, j)))
        args.append(res)
    out_spec = pl.BlockSpec((bm, bn), lambda i, j: (i, j))
    outs = pl.pallas_call(
        functools.partial(_mm_kernel, n_ops=len(a_list), norm=norm, act=act, has_res=res is not None,
                          n_out=len(out_dtypes)),
        grid=(m // bm, n // bn),
        in_specs=in_specs,
        out_specs=[out_spec] * len(out_dtypes),
        out_shape=[jax.ShapeDtypeStruct((m, n), d) for d in out_dtypes],
        scratch_shapes=[pltpu.VMEM((bm, V7X_LANES), F32)] if norm else [],
        compiler_params=_params("parallel", "arbitrary"),
        name=name,
    )(*args)
    return outs if len(outs) > 1 else outs[0]


def _mm_kred_kernel(a_ref, w_ref, res_ref, o_ref, *maybe_bf16_ref):
    kk = pl.program_id(2)

    @pl.when(kk == 0)
    def _():
        o_ref[...] = res_ref[...]

    o_ref[...] += jnp.dot(a_ref[...], w_ref[...], preferred_element_type=F32)

    for ob_ref in maybe_bf16_ref:
        @pl.when(kk == pl.num_programs(2) - 1)
        def _():
            ob_ref[...] = o_ref[...].astype(ob_ref.dtype)


def matmul_kred_res(a, w, res, *, layer, with_bf16, name="matmul_kred"):
    m, k = a.shape
    n = w.shape[2]
    bm = _pick(m, (1024, 512, 256))
    bn = _pick(n, (1024, 512))
    bk = _pick(k, (4096, 2048, 1024))
    out_spec = pl.BlockSpec((bm, bn), lambda i, j, kk: (i, j))
    out_shape = [jax.ShapeDtypeStruct((m, n), F32)]
    if with_bf16:
        out_shape.append(jax.ShapeDtypeStruct((m, n), BF16))
    outs = pl.pallas_call(
        _mm_kred_kernel,
        grid=(m // bm, n // bn, k // bk),
        in_specs=[pl.BlockSpec((bm, bk), lambda i, j, kk: (i, kk)),
                  pl.BlockSpec((None, bk, bn), lambda i, j, kk: (layer, kk, j)),
                  pl.BlockSpec((bm, bn), lambda i, j, kk: (i, j))],
        out_specs=[out_spec] * len(out_shape),
        out_shape=out_shape,
        compiler_params=_params("parallel", "parallel", "arbitrary"),
        name=name,
    )(a, w, res)
    return outs if with_bf16 else outs[0]


def _gate_kernel(u_ref, v_ref, vg_ref, ws_ref, bs_ref, o_ref, *, n_chunks, gw):
    inv = _row_inv_rms(v_ref)

    def gate(g, carry):
        c0 = pl.multiple_of(g * gw, gw)
        vn = ((v_ref[:, pl.ds(c0, gw)].astype(F32) * inv) * vg_ref[:, pl.ds(c0, gw)]).astype(BF16)
        u = u_ref[:, pl.ds(c0, gw)].astype(F32)
        w = ws_ref[g]
        b = bs_ref[g]
        for c in range(n_chunks):
            r0 = c * A_CHUNK
            vm = jnp.dot(w, vn[r0:r0 + A_CHUNK], preferred_element_type=F32) + b
            o_ref[r0:r0 + A_CHUNK, pl.ds(c0, gw)] = (u[r0:r0 + A_CHUNK] * vm).astype(o_ref.dtype)
        return carry

    lax.fori_loop(0, A_GROUPS, gate, 0)


def gated_chunk_mlp(uv, w_s, b_s, v_gain):
    m, w2 = uv.shape
    width = w2 // 2
    gw = width // A_GROUPS
    rows = _pick(m, (512, 256, 128))
    return pl.pallas_call(
        functools.partial(_gate_kernel, n_chunks=rows // A_CHUNK, gw=gw),
        grid=(m // rows,),
        in_specs=[pl.BlockSpec((rows, width), lambda i: (i, 0)),
                  pl.BlockSpec((rows, width), lambda i: (i, 1)),
                  pl.BlockSpec((1, width), lambda i: (0, 0)),
                  pl.BlockSpec((A_GROUPS, A_CHUNK, A_CHUNK), lambda i: (0, 0, 0)),
                  pl.BlockSpec((A_GROUPS, A_CHUNK, 1), lambda i: (0, 0, 0))],
        out_specs=pl.BlockSpec((rows, width), lambda i: (i, 0)),
        out_shape=jax.ShapeDtypeStruct((m, width), BF16),
        compiler_params=_params("parallel"),
        name="gated_chunk_mlp",
    )(uv, uv, v_gain.reshape(1, width).astype(F32), w_s.astype(BF16),
      b_s.reshape(A_GROUPS, A_CHUNK, 1).astype(F32))


def _dft_tables(n):
    idx = lax.iota(jnp.int32, n)
    prod = (idx[:, None] * idx[None, :]) % n
    theta = prod.astype(F32) * (2.0 * math.pi / n)
    scale = 1.0 / math.sqrt(n)
    return jnp.cos(theta) * scale, -jnp.sin(theta) * scale


def _half_channel_dft(gc):
    cos, nsin = _dft_tables(gc)
    half = gc // 2
    return jnp.concatenate([cos[:, :half + 1], -nsin[:, 1:half]], axis=1)


def _fourier_row_order(w, width):
    gc = width // B_GROUPS
    half = gc // 2
    parts = []
    for g in range(B_GROUPS):
        seg = w[g * gc:(g + 1) * gc]
        parts += [seg[:half + 1], jnp.flip(seg[half + 1:], axis=0)]
    return jnp.concatenate(parts + [w[width:]], axis=0)


def _chan_dft_kernel(z_ref, w_ref, o_ref):
    o_ref[...] = jnp.dot(z_ref[...], w_ref[...], preferred_element_type=F32).astype(o_ref.dtype)


def fold_channel_dft(w_in, width):
    k = w_in.shape[0]
    gc = width // B_GROUPS
    bm = _pick(k, (1024, 512, 256))
    return pl.pallas_call(
        _chan_dft_kernel,
        grid=(k // bm, B_GROUPS),
        in_specs=[pl.BlockSpec((bm, gc), lambda i, g: (i, g)),
                  pl.BlockSpec((gc, gc), lambda i, g: (0, 0))],
        out_specs=pl.BlockSpec((bm, gc), lambda i, g: (i, g)),
        out_shape=jax.ShapeDtypeStruct((k, width), BF16),
        compiler_params=_params("parallel", "parallel"),
        name="fold_channel_dft",
    )(w_in, _half_channel_dft(gc).astype(BF16))


def _pos_dft_kernel(c_ref, ns_ref, pc_ref, pq_ref, o_ref):
    half = pc_ref.shape[1]
    a = jnp.dot(c_ref[...], pc_ref[...], preferred_element_type=F32)
    b = jnp.dot(ns_ref[...], pq_ref[...], preferred_element_type=F32)
    nyq = jnp.dot(c_ref[...], pq_ref[:, :V7X_LANES], preferred_element_type=F32)[:, 0:1]
    first = lax.broadcasted_iota(jnp.int32, a.shape, 1) == 0
    o_ref[:, :half] = (a + jnp.where(first, 0.0, b)).astype(o_ref.dtype)
    o_ref[:, half:] = jnp.where(first, nyq, a - b).astype(o_ref.dtype)


def fourier_positions(p, n_seq, seq):
    m, width = p.shape
    gc = width // B_GROUPS
    half = gc // 2
    cos, nsin = _dft_tables(seq)
    bo = _pick(seq, (1024, 512, 256))
    nb = seq // bo
    return pl.pallas_call(
        _pos_dft_kernel,
        grid=(n_seq, B_GROUPS, nb),
        in_specs=[pl.BlockSpec((bo, seq), lambda b, g, i: (i, 0)),
                  pl.BlockSpec((bo, seq), lambda b, g, i: (i, 0)),
                  pl.BlockSpec((seq, half), lambda b, g, i: (b, 2 * g)),
                  pl.BlockSpec((seq, half), lambda b, g, i: (b, 2 * g + 1))],
        out_specs=pl.BlockSpec((bo, gc), lambda b, g, i: (b * nb + i, g)),
        out_shape=jax.ShapeDtypeStruct((m, width), BF16),
        compiler_params=_params("parallel", "parallel", "arbitrary"),
        name="fourier_positions",
    )(cos.astype(BF16), nsin.astype(BF16), p, p)


def _head_norm(x, g):
    x = x.astype(F32)
    ms = jnp.mean(x * x, axis=-1, keepdims=True)
    return (x * lax.rsqrt(ms + EPS)) * g


def _band_bias(n_blocks, qblk, radius, dilation, slope, length):
    kw = qblk + 2 * radius
    row = lax.broadcasted_iota(jnp.int32, (qblk, kw), 0)
    col = lax.broadcasted_iota(jnp.int32, (qblk, kw), 1)
    dist = jnp.abs(row - (col - radius))
    inner = jnp.where(dist <= radius, -slope * (dist * dilation).astype(F32), NEG_INF)
    first = jnp.where(col < radius, NEG_INF, inner)
    last_col = length - (n_blocks - 1) * qblk + radius
    last = jnp.where(col >= last_col, NEG_INF, inner)
    if n_blocks == 1:
        return jnp.where(col >= last_col, NEG_INF, first)[None]
    return jnp.stack([first] + [inner] * (n_blocks - 2) + [last])


def _windows(x, n_blocks, qblk, radius):
    pad = jnp.zeros((radius, x.shape[1]), x.dtype)
    xp = jnp.concatenate([pad, x, pad], axis=0)
    return jnp.stack([xp[t * qblk:t * qblk + qblk + 2 * radius] for t in range(n_blocks)])


def _block_softmax_pv(q, k, v, bias):
    scores = jnp.einsum("bqc,bkc->bqk", q, k, preferred_element_type=F32) * (HEAD_DIM ** -0.5)
    logits = scores + bias
    mx = jnp.max(logits, axis=-1, keepdims=True)
    p = jnp.exp(logits - mx)
    den = jnp.sum(p, axis=-1, keepdims=True)
    pv = jnp.einsum("bqk,bkc->bqc", p.astype(BF16), v, preferred_element_type=F32)
    return mx, den, pv


def _dil_attn_kernel(*refs, seq):
    qkv_refs = refs[:3 * C_GROUPS]
    qg_ref, kg_ref, sl_ref, o_ref, q_s, k_s, v_s, m_s, l_s, acc_s = refs[3 * C_GROUPS:]
    slope = sl_ref[:, 0:1]
    lanes = (HEAD_DIM,)
    for g, (window, dilation) in enumerate(C_PAIRS):
        radius = window // (2 * dilation)
        length = seq // dilation
        qblk = min(length, 256)
        nb = length // qblk
        q_ref, k_ref, v_ref = qkv_refs[3 * g:3 * g + 3]
        q_s[...] = _head_norm(q_ref[...], qg_ref[g:g + 1, :])
        k_s[...] = _head_norm(k_ref[...], kg_ref[g:g + 1, :])
        v_s[...] = v_ref[...].astype(F32)

        def residue(ref, r):
            return ref[pl.ds(r, length, stride=dilation), :] if dilation > 1 else ref[...]

        qb = jnp.concatenate([residue(q_s, r).astype(BF16).reshape(nb, qblk, HEAD_DIM) for r in range(dilation)])
        kb = jnp.concatenate([_windows(residue(k_s, r).astype(BF16), nb, qblk, radius) for r in range(dilation)])
        vb = jnp.concatenate([_windows(residue(v_s, r).astype(BF16), nb, qblk, radius) for r in range(dilation)])
        bias = _band_bias(nb, qblk, radius, dilation, slope, length)
        bias = jnp.concatenate([bias] * dilation)
        mx, den, pv = _block_softmax_pv(qb, kb, vb, bias)
        mx = jnp.broadcast_to(mx, mx.shape[:2] + lanes)
        den = jnp.broadcast_to(den, den.shape[:2] + lanes)
        if g == 0:
            m_s[...] = mx.reshape(seq, HEAD_DIM)
            l_s[...] = den.reshape(seq, HEAD_DIM)
            acc_s[...] = pv.reshape(seq, HEAD_DIM)
            continue
        for r in range(dilation):
            rows = slice(r * nb, (r + 1) * nb)
            m_old = residue(m_s, r)
            m_blk = mx[rows].reshape(length, HEAD_DIM)
            m_new = jnp.maximum(m_old, m_blk)
            a = jnp.exp(m_old - m_new)
            b = jnp.exp(m_blk - m_new)
            l_new = a * residue(l_s, r) + b * den[rows].reshape(length, HEAD_DIM)
            acc_new = a * residue(acc_s, r) + b * pv[rows].reshape(length, HEAD_DIM)
            m_s[pl.ds(r, length, stride=dilation), :] = m_new
            l_s[pl.ds(r, length, stride=dilation), :] = l_new
            acc_s[pl.ds(r, length, stride=dilation), :] = acc_new
    o_ref[...] = (acc_s[...] / l_s[...]).astype(o_ref.dtype)


def dilated_mixture(z, q_gain, k_gain, n_seq, seq):
    for window, dilation in C_PAIRS:
        assert seq % (dilation * (window // (2 * dilation))) == 0
    slopes = 2.0 ** (-ALIBI_MAX * jnp.arange(1, C_HEADS + 1, dtype=F32) / C_HEADS)
    slopes = jnp.repeat(slopes, HEAD_DIM).reshape(1, C_WIDTH)
    heads_per_part = C_WIDTH // HEAD_DIM

    def qkv_spec(g, part):
        col = (g * 3 + part) * heads_per_part
        return pl.BlockSpec((seq, HEAD_DIM), lambda b, h: (b, col + h))

    gain_spec = pl.BlockSpec((C_GROUPS, HEAD_DIM), lambda b, h: (0, 0))
    state = pltpu.VMEM((seq, HEAD_DIM), F32)
    return pl.pallas_call(
        functools.partial(_dil_attn_kernel, seq=seq),
        grid=(n_seq, C_HEADS),
        in_specs=[qkv_spec(g, part) for g in range(C_GROUPS) for part in range(3)]
                 + [gain_spec, gain_spec, pl.BlockSpec((1, HEAD_DIM), lambda b, h: (0, h))],
        out_specs=pl.BlockSpec((seq, HEAD_DIM), lambda b, h: (b, h)),
        out_shape=jax.ShapeDtypeStruct((n_seq * seq, C_WIDTH), BF16),
        scratch_shapes=[state] * 6,
        compiler_params=_params("parallel", "parallel"),
        name="dilated_attention",
    )(*([z] * (3 * C_GROUPS)), q_gain.astype(F32), k_gain.astype(F32), slopes)


def _mem_attn_kernel(q_ref, kv_ref, qg_ref, kg_ref, o_ref):
    scale = HEAD_DIM ** -0.5
    for h in range(MEM_HEADS):
        c0 = h * HEAD_DIM
        qn = _head_norm(q_ref[:, c0:c0 + HEAD_DIM], qg_ref[...]).astype(BF16)
        kn = _head_norm(kv_ref[:, c0:c0 + HEAD_DIM], kg_ref[...]).astype(BF16)
        v = kv_ref[:, MEM_WIDTH + c0:MEM_WIDTH + c0 + HEAD_DIM].astype(BF16)
        s = lax.dot_general(qn, kn, (((1,), (1,)), ((), ())), preferred_element_type=F32) * scale
        mx = jnp.max(s, axis=-1, keepdims=True)
        e = jnp.exp(s - mx)
        p = e / jnp.sum(e, axis=-1, keepdims=True)
        o_ref[:, c0:c0 + HEAD_DIM] = jnp.dot(p.astype(BF16), v, preferred_element_type=F32).astype(o_ref.dtype)


def memory_attention(q, kv, q_gain, k_gain, n_seq, seq, n_mem):
    bm = _pick(seq, (1024, 512, 256))
    nb = seq // bm
    gain_spec = pl.BlockSpec((1, HEAD_DIM), lambda b, i: (0, 0))
    return pl.pallas_call(
        _mem_attn_kernel,
        grid=(n_seq, nb),
        in_specs=[pl.BlockSpec((bm, MEM_WIDTH), lambda b, i: (b * nb + i, 0)),
                  pl.BlockSpec((n_mem, 2 * MEM_WIDTH), lambda b, i: (b, 0)),
                  gain_spec, gain_spec],
        out_specs=pl.BlockSpec((bm, MEM_WIDTH), lambda b, i: (b * nb + i, 0)),
        out_shape=jax.ShapeDtypeStruct((n_seq * seq, MEM_WIDTH), BF16),
        compiler_params=_params("parallel", "arbitrary"),
        name="memory_attention",
    )(q, kv, q_gain.reshape(1, HEAD_DIM).astype(F32), k_gain.reshape(1, HEAD_DIM).astype(F32))


def _trunk(x, mem, p, depth):
    n_seq, seq, d = x.shape
    n_mem = mem.shape[1]
    x = x.reshape(n_seq * seq, d)
    xb = x.astype(BF16)
    memb = mem.reshape(n_seq * n_mem, d).astype(BF16)
    for i in range(depth):
        kind, j = i % N_MIXERS, i // N_MIXERS
        w_in, w_out = p["w_in"][kind], p["w_out"][kind]
        y_width = w_in.shape[2] - MEM_WIDTH
        z = matmul([xb], [p["b_w_in_dft"] if kind == 1 else w_in], layer=j, out_dtypes=[BF16], n_cols=y_width,
                   norm=True, act="gelu" if kind == 0 else None, name="in_proj")
        q_mem = matmul([xb], [w_in], layer=j, out_dtypes=[BF16], n_cols=MEM_WIDTH, col0=y_width, norm=True,
                       name="in_proj_mem")
        if kind == 0:
            y = gated_chunk_mlp(z, p["a_w_s"][j], p["a_b_s"][j], p["a_v_gain"][j])
        elif kind == 1:
            y = fourier_positions(z, n_seq, seq)
        else:
            y = dilated_mixture(z, p["c_q_gain"][j], p["c_k_gain"][j], n_seq, seq)
        kv = matmul([memb], [p["mem_w_kv"]], layer=i, out_dtypes=[F32], norm=True, name="mem_kv")
        mo = memory_attention(q_mem, kv, p["mem_q_gain"][i], p["mem_k_gain"][i], n_seq, seq, n_mem)
        x, xb = matmul([y, mo], [w_out, w_out], layer=j, out_dtypes=[F32, BF16], row0s=[0, y.shape[1]], res=x,
                       name="out_proj")
        f = matmul([xb], [p["w_ff1"]], layer=i, out_dtypes=[BF16], norm=True, act="relu2", name="ffn_up")
        if i + 1 < depth:
            x, xb = matmul_kred_res(f, p["w_ff2"], x, layer=i, with_bf16=True, name="ffn_down")
        else:
            x = matmul_kred_res(f, p["w_ff2"], x, layer=i, with_bf16=False, name="ffn_down")
    return x.reshape(n_seq, seq, d)


def _fold_gain(w, g):
    return (g.astype(F32)[:, :, None] * w).astype(BF16)


@jax.jit
def kernel(x_prompt, x_sample, mem_prompt, mem_sample, mixer_norm, mem_norm, ffn_norm, mem_w_kv,
           mem_q_gain, mem_k_gain, w_ff1, w_ff2, a_w_in, a_w_out, a_w_s, a_b_s, a_v_gain,
           b_w_in, b_w_out, c_w_in, c_w_out, c_q_gain, c_k_gain):
    depth = mixer_norm.shape[0]
    w_in = [_fold_gain(w, mixer_norm[k::N_MIXERS]) for k, w in enumerate((a_w_in, b_w_in, c_w_in))]
    b_width = b_w_in.shape[2] - MEM_WIDTH
    b_w_out_ordered = jnp.stack([_fourier_row_order(w, b_width) for w in b_w_out])
    p = {
        "mem_q_gain": mem_q_gain, "mem_k_gain": mem_k_gain,
        "a_w_s": a_w_s, "a_b_s": a_b_s, "a_v_gain": a_v_gain,
        "c_q_gain": c_q_gain, "c_k_gain": c_k_gain,
        "mem_w_kv": _fold_gain(mem_w_kv, mem_norm),
        "w_ff1": _fold_gain(w_ff1, ffn_norm),
        "w_ff2": w_ff2.astype(BF16),
        "w_in": w_in,
        "b_w_in_dft": jnp.stack([fold_channel_dft(w, b_width) for w in w_in[1]]),
        "w_out": [a_w_out.astype(BF16), b_w_out_ordered.astype(BF16), c_w_out.astype(BF16)],
    }
    return _trunk(x_prompt, mem_prompt, p, depth), _trunk(x_sample, mem_sample, p, depth)
```

```python
import functools
import math

import jax
import jax.numpy as jnp
from jax import lax
from jax.experimental import pallas as pl
from jax.experimental.pallas import tpu as pltpu

F32 = jnp.float32
BF16 = jnp.bfloat16

EPS = 1e-6
NEG_INF = -1e30
HEAD_DIM = 128
MEM_HEADS = 4
MEM_WIDTH = MEM_HEADS * HEAD_DIM
A_CHUNK = 128
A_GROUPS = 8
B_GROUPS = 4
C_PAIRS = ((128, 1), (512, 4), (2048, 16))
C_GROUPS = len(C_PAIRS)
C_HEADS = 8
C_WIDTH = C_HEADS * HEAD_DIM
ALIBI_MAX = 8.0
N_MIXERS = 3

V7X_VMEM_BYTES = 64 * 1024 * 1024
VMEM_LIMIT_BYTES = V7X_VMEM_BYTES - 2 * 1024 * 1024
V7X_LANES = 128


def _params(*semantics):
    return pltpu.CompilerParams(dimension_semantics=semantics, vmem_limit_bytes=VMEM_LIMIT_BYTES)


def _pick(n, candidates):
    for c in candidates:
        if n % c == 0:
            return c
    raise ValueError(f"no block size in {candidates} divides {n}")


def _gelu(x):
    return 0.5 * x * (1.0 + lax.erf(x * (1.0 / math.sqrt(2.0))))


def _row_inv_rms(a_ref):
    rows, k = a_ref.shape
    part = jnp.zeros((rows, V7X_LANES), F32)
    for c in range(k // V7X_LANES):
        blk = a_ref[:, c * V7X_LANES:(c + 1) * V7X_LANES].astype(F32)
        part = part + blk * blk
    return lax.rsqrt(jnp.sum(part, axis=-1, keepdims=True) / k + EPS)


def _mm_kernel(*refs, n_ops, norm, act, has_res, n_out, side):
    a_refs = refs[:n_ops]
    w_refs = refs[n_ops:2 * n_ops]
    pos = 2 * n_ops
    ws_ref = refs[pos] if side else None
    pos += side
    res_ref = refs[pos] if has_res else None
    pos += has_res
    o_refs = refs[pos:pos + n_out]
    pos += n_out
    os_ref = refs[pos] if side else None
    pos += side
    if norm:
        inv_ref = refs[pos]

        @pl.when(pl.program_id(1) == 0)
        def _():
            inv = _row_inv_rms(a_refs[0])
            inv_ref[...] = jnp.broadcast_to(inv, inv_ref.shape)
            if side:
                d = jnp.dot(a_refs[0][...], ws_ref[...], preferred_element_type=F32)
                os_ref[...] = (d * inv).astype(os_ref.dtype)

    acc = None
    for a_ref, w_ref in zip(a_refs, w_refs):
        d = jnp.dot(a_ref[...], w_ref[...], preferred_element_type=F32)
        acc = d if acc is None else acc + d
    if norm:
        acc = acc * inv_ref[:, 0:1]
    if act == "relu2":
        r = jnp.maximum(acc, 0.0)
        acc = r * r
    elif act == "gelu":
        acc = _gelu(acc)
    if has_res:
        acc = res_ref[...] + acc
    for o_ref in o_refs:
        o_ref[...] = acc.astype(o_ref.dtype)


def _mm_blocks(m, n, k_total, out_bytes, has_res, side_cols):
    for bm, bn in ((1024, 1024), (1024, 512), (512, 1024), (512, 512), (256, 512)):
        if m % bm or n % bn:
            continue
        windows = 2 * (bm * k_total * 2 + k_total * bn * 2 + bm * bn * (out_bytes + 4 * has_res)
                       + (k_total + bm) * side_cols * 2)
        if windows + 3 * bm * bn * 4 <= VMEM_LIMIT_BYTES:
            return bm, bn
    raise ValueError(f"no matmul blocks fit m={m} n={n} k={k_total}")


def matmul(a_list, w_list, *, out_dtypes, layer=0, n_cols=None, col0=0, row0s=None, norm=False, act=None,
           res=None, side=None, name="matmul"):
    m = a_list[0].shape[0]
    n = n_cols if n_cols is not None else w_list[0].shape[2]
    row0s = row0s or [0] * len(a_list)
    ks = [a.shape[1] for a in a_list]
    out_bytes = sum(jnp.dtype(d).itemsize for d in out_dtypes)
    side_cols = side[2] if side else 0
    bm, bn = _mm_blocks(m, n, sum(ks), out_bytes, res is not None, side_cols)
    assert col0 % bn == 0 and all(r % k == 0 for r, k in zip(row0s, ks))
    assert not side or (norm and len(a_list) == 1 and side[1] % side[2] == 0)
    cb = col0 // bn
    in_specs = [pl.BlockSpec((bm, k), lambda i, j: (i, 0)) for k in ks]
    in_specs += [pl.BlockSpec((None, k, bn), functools.partial(lambda i, j, rb: (layer, rb, cb + j), rb=r // k))
                 for r, k in zip(row0s, ks)]
    args = list(a_list) + list(w_list)
    out_specs = [pl.BlockSpec((bm, bn), lambda i, j: (i, j))] * len(out_dtypes)
    out_shape = [jax.ShapeDtypeStruct((m, n), d) for d in out_dtypes]
    if side:
        sb = side[1] // side_cols
        in_specs.append(pl.BlockSpec((None, ks[0], side_cols), lambda i, j: (layer, 0, sb)))
        args.append(side[0])
        out_specs.append(pl.BlockSpec((bm, side_cols), lambda i, j: (i, 0)))
        out_shape.append(jax.ShapeDtypeStruct((m, side_cols), BF16))
    if res is not None:
        in_specs.append(pl.BlockSpec((bm, bn), lambda i, j: (i, j)))
        args.append(res)
    outs = pl.pallas_call(
        functools.partial(_mm_kernel, n_ops=len(a_list), norm=norm, act=act, has_res=res is not None,
                          n_out=len(out_dtypes), side=bool(side)),
        grid=(m // bm, n // bn),
        in_specs=in_specs,
        out_specs=out_specs,
        out_shape=out_shape,
        scratch_shapes=[pltpu.VMEM((bm, V7X_LANES), F32)] if norm else [],
        compiler_params=_params("parallel", "arbitrary"),
        name=name,
    )(*args)
    return outs if len(outs) > 1 else outs[0]


def _mm_kred_kernel(a_ref, w_ref, res_ref, o_ref, *maybe_bf16_ref):
    kk = pl.program_id(2)

    @pl.when(kk == 0)
    def _():
        o_ref[...] = res_ref[...]

    o_ref[...] += jnp.dot(a_ref[...], w_ref[...], preferred_element_type=F32)

    for ob_ref in maybe_bf16_ref:
        @pl.when(kk == pl.num_programs(2) - 1)
        def _():
            ob_ref[...] = o_ref[...].astype(ob_ref.dtype)


def matmul_kred_res(a, w, res, *, layer, with_bf16, name="matmul_kred"):
    m, k = a.shape
    n = w.shape[2]
    bm = _pick(m, (1024, 512, 256))
    bn = _pick(n, (1024, 512))
    bk = _pick(k, (4096, 2048, 1024))
    out_spec = pl.BlockSpec((bm, bn), lambda i, j, kk: (i, j))
    out_shape = [jax.ShapeDtypeStruct((m, n), F32)]
    if with_bf16:
        out_shape.append(jax.ShapeDtypeStruct((m, n), BF16))
    outs = pl.pallas_call(
        _mm_kred_kernel,
        grid=(m // bm, n // bn, k // bk),
        in_specs=[pl.BlockSpec((bm, bk), lambda i, j, kk: (i, kk)),
                  pl.BlockSpec((None, bk, bn), lambda i, j, kk: (layer, kk, j)),
                  pl.BlockSpec((bm, bn), lambda i, j, kk: (i, j))],
        out_specs=[out_spec] * len(out_shape),
        out_shape=out_shape,
        compiler_params=_params("parallel", "parallel", "arbitrary"),
        name=name,
    )(a, w, res)
    return outs if with_bf16 else outs[0]


def _gate_kernel(u_ref, v_ref, vg_ref, ws_ref, bs_ref, o_ref, *, n_chunks, gw):
    inv = _row_inv_rms(v_ref)

    def gate(g, carry):
        c0 = pl.multiple_of(g * gw, gw)
        vn = ((v_ref[:, pl.ds(c0, gw)].astype(F32) * inv) * vg_ref[:, pl.ds(c0, gw)]).astype(BF16)
        u = u_ref[:, pl.ds(c0, gw)].astype(F32)
        w = ws_ref[g]
        b = bs_ref[g]
        for c in range(n_chunks):
            r0 = c * A_CHUNK
            vm = jnp.dot(w, vn[r0:r0 + A_CHUNK], preferred_element_type=F32) + b
            o_ref[r0:r0 + A_CHUNK, pl.ds(c0, gw)] = (u[r0:r0 + A_CHUNK] * vm).astype(o_ref.dtype)
        return carry

    lax.fori_loop(0, A_GROUPS, gate, 0)


def gated_chunk_mlp(uv, w_s, b_s, v_gain):
    m, w2 = uv.shape
    width = w2 // 2
    gw = width // A_GROUPS
    rows = _pick(m, (512, 256, 128))
    return pl.pallas_call(
        functools.partial(_gate_kernel, n_chunks=rows // A_CHUNK, gw=gw),
        grid=(m // rows,),
        in_specs=[pl.BlockSpec((rows, width), lambda i: (i, 0)),
                  pl.BlockSpec((rows, width), lambda i: (i, 1)),
                  pl.BlockSpec((1, width), lambda i: (0, 0)),
                  pl.BlockSpec((A_GROUPS, A_CHUNK, A_CHUNK), lambda i: (0, 0, 0)),
                  pl.BlockSpec((A_GROUPS, A_CHUNK, 1), lambda i: (0, 0, 0))],
        out_specs=pl.BlockSpec((rows, width), lambda i: (i, 0)),
        out_shape=jax.ShapeDtypeStruct((m, width), BF16),
        compiler_params=_params("parallel"),
        name="gated_chunk_mlp",
    )(uv, uv, v_gain.reshape(1, width).astype(F32), w_s.astype(BF16),
      b_s.reshape(A_GROUPS, A_CHUNK, 1).astype(F32))


def _dft_tables(n):
    idx = lax.iota(jnp.int32, n)
    prod = (idx[:, None] * idx[None, :]) % n
    theta = prod.astype(F32) * (2.0 * math.pi / n)
    scale = 1.0 / math.sqrt(n)
    return jnp.cos(theta) * scale, -jnp.sin(theta) * scale


def _half_channel_dft(gc):
    cos, nsin = _dft_tables(gc)
    half = gc // 2
    return jnp.concatenate([cos[:, :half + 1], -nsin[:, 1:half]], axis=1)


def _fourier_channel_order(gc):
    half = gc // 2
    return list(range(half + 1)) + list(range(gc - 1, half, -1))


def fourier_row_order(w_out, width):
    n = w_out.shape[1]
    gc = width // B_GROUPS
    order = jnp.asarray(_fourier_channel_order(gc), jnp.int32)
    onehot = (order[:, None] == lax.iota(jnp.int32, gc)[None, :]).astype(BF16)
    bn = _pick(n, (1024, 512))
    return pl.pallas_call(
        _chan_dft_kernel,
        grid=(B_GROUPS, n // bn),
        in_specs=[pl.BlockSpec((gc, gc), lambda g, j: (0, 0)),
                  pl.BlockSpec((gc, bn), lambda g, j: (g, j))],
        out_specs=pl.BlockSpec((gc, bn), lambda g, j: (g, j)),
        out_shape=jax.ShapeDtypeStruct((width, n), BF16),
        compiler_params=_params("parallel", "parallel"),
        name="fourier_row_order",
    )(onehot, w_out)


def _chan_dft_kernel(z_ref, w_ref, o_ref):
    o_ref[...] = jnp.dot(z_ref[...], w_ref[...], preferred_element_type=F32).astype(o_ref.dtype)


def fold_channel_dft(w_in, width):
    k = w_in.shape[0]
    gc = width // B_GROUPS
    bm = _pick(k, (1024, 512, 256))
    return pl.pallas_call(
        _chan_dft_kernel,
        grid=(k // bm, B_GROUPS),
        in_specs=[pl.BlockSpec((bm, gc), lambda i, g: (i, g)),
                  pl.BlockSpec((gc, gc), lambda i, g: (0, 0))],
        out_specs=pl.BlockSpec((bm, gc), lambda i, g: (i, g)),
        out_shape=jax.ShapeDtypeStruct((k, width), BF16),
        compiler_params=_params("parallel", "parallel"),
        name="fold_channel_dft",
    )(w_in, _half_channel_dft(gc).astype(BF16))


def _pos_dft_kernel(c_ref, ns_ref, pc_ref, pq_ref, o_ref):
    half = pc_ref.shape[1]
    a = jnp.dot(c_ref[...], pc_ref[...], preferred_element_type=F32)
    b = jnp.dot(ns_ref[...], pq_ref[...], preferred_element_type=F32)
    nyq = jnp.dot(c_ref[...], pq_ref[:, :V7X_LANES], preferred_element_type=F32)[:, 0:1]
    first = lax.broadcasted_iota(jnp.int32, a.shape, 1) == 0
    o_ref[:, :half] = (a + jnp.where(first, 0.0, b)).astype(o_ref.dtype)
    o_ref[:, half:] = jnp.where(first, nyq, a - b).astype(o_ref.dtype)


def fourier_positions(p, n_seq, seq):
    m, width = p.shape
    gc = width // B_GROUPS
    half = gc // 2
    cos, nsin = _dft_tables(seq)
    bo = _pick(seq, (1024, 512, 256))
    nb = seq // bo
    return pl.pallas_call(
        _pos_dft_kernel,
        grid=(n_seq, B_GROUPS, nb),
        in_specs=[pl.BlockSpec((bo, seq), lambda b, g, i: (i, 0)),
                  pl.BlockSpec((bo, seq), lambda b, g, i: (i, 0)),
                  pl.BlockSpec((seq, half), lambda b, g, i: (b, 2 * g)),
                  pl.BlockSpec((seq, half), lambda b, g, i: (b, 2 * g + 1))],
        out_specs=pl.BlockSpec((bo, gc), lambda b, g, i: (b * nb + i, g)),
        out_shape=jax.ShapeDtypeStruct((m, width), BF16),
        compiler_params=_params("parallel", "parallel", "arbitrary"),
        name="fourier_positions",
    )(cos.astype(BF16), nsin.astype(BF16), p, p)


def _head_norm(x, g):
    x = x.astype(F32)
    ms = jnp.mean(x * x, axis=-1, keepdims=True)
    return (x * lax.rsqrt(ms + EPS)) * g


def _band_bias(n_blocks, qblk, radius, dilation, slope, length):
    kw = qblk + 2 * radius
    row = lax.broadcasted_iota(jnp.int32, (qblk, kw), 0)
    col = lax.broadcasted_iota(jnp.int32, (qblk, kw), 1)
    dist = jnp.abs(row - (col - radius))
    inner = jnp.where(dist <= radius, -slope * (dist * dilation).astype(F32), NEG_INF)
    first = jnp.where(col < radius, NEG_INF, inner)
    last_col = length - (n_blocks - 1) * qblk + radius
    last = jnp.where(col >= last_col, NEG_INF, inner)
    if n_blocks == 1:
        return jnp.where(col >= last_col, NEG_INF, first)[None]
    return jnp.stack([first] + [inner] * (n_blocks - 2) + [last])


def _windows(x, n_blocks, qblk, radius):
    pad = jnp.zeros((radius, x.shape[1]), x.dtype)
    xp = jnp.concatenate([pad, x, pad], axis=0)
    return jnp.stack([xp[t * qblk:t * qblk + qblk + 2 * radius] for t in range(n_blocks)])


def _block_softmax_pv(q, k, v, bias):
    scores = jnp.einsum("bqc,bkc->bqk", q, k, preferred_element_type=F32) * (HEAD_DIM ** -0.5)
    logits = scores + bias
    mx = jnp.max(logits, axis=-1, keepdims=True)
    p = jnp.exp(logits - mx)
    den = jnp.sum(p, axis=-1, keepdims=True)
    pv = jnp.einsum("bqk,bkc->bqc", p.astype(BF16), v, preferred_element_type=F32)
    return mx, den, pv


def _dil_attn_kernel(*refs, seq):
    qkv_refs = refs[:3 * C_GROUPS]
    qg_ref, kg_ref, sl_ref, o_ref, q_s, k_s, v_s, m_s, l_s, acc_s = refs[3 * C_GROUPS:]
    slope = sl_ref[:, 0:1]
    lanes = (HEAD_DIM,)
    for g, (window, dilation) in enumerate(C_PAIRS):
        radius = window // (2 * dilation)
        length = seq // dilation
        qblk = min(length, 128)
        nb = length // qblk
        q_ref, k_ref, v_ref = qkv_refs[3 * g:3 * g + 3]
        qn = _head_norm(q_ref[...], qg_ref[g:g + 1, :])
        kn = _head_norm(k_ref[...], kg_ref[g:g + 1, :])

        def residue(ref, r):
            return ref[pl.ds(r, length, stride=dilation), :]

        if dilation == 1:
            qr, kr, vr = [qn.astype(BF16)], [kn.astype(BF16)], [v_ref[...]]
        else:
            q_s[...] = qn
            k_s[...] = kn
            v_s[...] = v_ref[...].astype(F32)
            qr = [residue(q_s, r).astype(BF16) for r in range(dilation)]
            kr = [residue(k_s, r).astype(BF16) for r in range(dilation)]
            vr = [residue(v_s, r).astype(BF16) for r in range(dilation)]
        qb = jnp.concatenate([x.reshape(nb, qblk, HEAD_DIM) for x in qr])
        kb = jnp.concatenate([_windows(x, nb, qblk, radius) for x in kr])
        vb = jnp.concatenate([_windows(x, nb, qblk, radius) for x in vr])
        bias = _band_bias(nb, qblk, radius, dilation, slope, length)
        bias = jnp.concatenate([bias] * dilation)
        mx, den, pv = _block_softmax_pv(qb, kb, vb, bias)
        mx = jnp.broadcast_to(mx, mx.shape[:2] + lanes)
        den = jnp.broadcast_to(den, den.shape[:2] + lanes)
        if g == 0:
            m_s[...] = mx.reshape(seq, HEAD_DIM)
            l_s[...] = den.reshape(seq, HEAD_DIM)
            acc_s[...] = pv.reshape(seq, HEAD_DIM)
            continue
        for r in range(dilation):
            rows = slice(r * nb, (r + 1) * nb)
            m_old = residue(m_s, r)
            m_blk = mx[rows].reshape(length, HEAD_DIM)
            m_new = jnp.maximum(m_old, m_blk)
            a = jnp.exp(m_old - m_new)
            b = jnp.exp(m_blk - m_new)
            l_new = a * residue(l_s, r) + b * den[rows].reshape(length, HEAD_DIM)
            acc_new = a * residue(acc_s, r) + b * pv[rows].reshape(length, HEAD_DIM)
            m_s[pl.ds(r, length, stride=dilation), :] = m_new
            l_s[pl.ds(r, length, stride=dilation), :] = l_new
            acc_s[pl.ds(r, length, stride=dilation), :] = acc_new
    o_ref[...] = (acc_s[...] / l_s[...]).astype(o_ref.dtype)


def dilated_mixture(z, q_gain, k_gain, n_seq, seq):
    for window, dilation in C_PAIRS:
        assert seq % (dilation * (window // (2 * dilation))) == 0
    slopes = 2.0 ** (-ALIBI_MAX * jnp.arange(1, C_HEADS + 1, dtype=F32) / C_HEADS)
    slopes = jnp.repeat(slopes, HEAD_DIM).reshape(1, C_WIDTH)
    heads_per_part = C_WIDTH // HEAD_DIM

    def qkv_spec(g, part):
        col = (g * 3 + part) * heads_per_part
        return pl.BlockSpec((seq, HEAD_DIM), lambda b, h: (b, col + h))

    gain_spec = pl.BlockSpec((C_GROUPS, HEAD_DIM), lambda b, h: (0, 0))
    state = pltpu.VMEM((seq, HEAD_DIM), F32)
    return pl.pallas_call(
        functools.partial(_dil_attn_kernel, seq=seq),
        grid=(n_seq, C_HEADS),
        in_specs=[qkv_spec(g, part) for g in range(C_GROUPS) for part in range(3)]
                 + [gain_spec, gain_spec, pl.BlockSpec((1, HEAD_DIM), lambda b, h: (0, h))],
        out_specs=pl.BlockSpec((seq, HEAD_DIM), lambda b, h: (b, h)),
        out_shape=jax.ShapeDtypeStruct((n_seq * seq, C_WIDTH), BF16),
        scratch_shapes=[state] * 6,
        compiler_params=_params("parallel", "parallel"),
        name="dilated_attention",
    )(*([z] * (3 * C_GROUPS)), q_gain.astype(F32), k_gain.astype(F32), slopes)


def _mem_attn_kernel(q_ref, kv_ref, qg_ref, kg_ref, o_ref):
    scale = HEAD_DIM ** -0.5
    for h in range(MEM_HEADS):
        c0 = h * HEAD_DIM
        qn = _head_norm(q_ref[:, c0:c0 + HEAD_DIM], qg_ref[...]).astype(BF16)
        kn = _head_norm(kv_ref[:, c0:c0 + HEAD_DIM], kg_ref[...]).astype(BF16)
        v = kv_ref[:, MEM_WIDTH + c0:MEM_WIDTH + c0 + HEAD_DIM].astype(BF16)
        s = lax.dot_general(qn, kn, (((1,), (1,)), ((), ())), preferred_element_type=F32) * scale
        mx = jnp.max(s, axis=-1, keepdims=True)
        e = jnp.exp(s - mx)
        p = e / jnp.sum(e, axis=-1, keepdims=True)
        o_ref[:, c0:c0 + HEAD_DIM] = jnp.dot(p.astype(BF16), v, preferred_element_type=F32).astype(o_ref.dtype)


def memory_attention(q, kv, q_gain, k_gain, n_seq, seq, n_mem):
    bm = _pick(seq, (1024, 512, 256))
    nb = seq // bm
    gain_spec = pl.BlockSpec((1, HEAD_DIM), lambda b, i: (0, 0))
    return pl.pallas_call(
        _mem_attn_kernel,
        grid=(n_seq, nb),
        in_specs=[pl.BlockSpec((bm, MEM_WIDTH), lambda b, i: (b * nb + i, 0)),
                  pl.BlockSpec((n_mem, 2 * MEM_WIDTH), lambda b, i: (b, 0)),
                  gain_spec, gain_spec],
        out_specs=pl.BlockSpec((bm, MEM_WIDTH), lambda b, i: (b * nb + i, 0)),
        out_shape=jax.ShapeDtypeStruct((n_seq * seq, MEM_WIDTH), BF16),
        compiler_params=_params("parallel", "arbitrary"),
        name="memory_attention",
    )(q, kv, q_gain.reshape(1, HEAD_DIM).astype(F32), k_gain.reshape(1, HEAD_DIM).astype(F32))


def _trunk(x, mem, p, depth):
    n_seq, seq, d = x.shape
    n_mem = mem.shape[1]
    x = x.reshape(n_seq * seq, d)
    xb = x.astype(BF16)
    memb = mem.reshape(n_seq * n_mem, d).astype(BF16)
    for i in range(depth):
        kind, j = i % N_MIXERS, i // N_MIXERS
        w_in, w_out = p["w_in"][kind], p["w_out"][kind]
        y_width = w_in.shape[2] - MEM_WIDTH
        z, q_mem = matmul([xb], [p["b_w_in_dft"] if kind == 1 else w_in], layer=j, out_dtypes=[BF16],
                          n_cols=y_width, norm=True, act="gelu" if kind == 0 else None,
                          side=(w_in, y_width, MEM_WIDTH), name="in_proj")
        if kind == 0:
            y = gated_chunk_mlp(z, p["a_w_s"][j], p["a_b_s"][j], p["a_v_gain"][j])
        elif kind == 1:
            y = fourier_positions(z, n_seq, seq)
        else:
            y = dilated_mixture(z, p["c_q_gain"][j], p["c_k_gain"][j], n_seq, seq)
        kv = matmul([memb], [p["mem_w_kv"]], layer=i, out_dtypes=[F32], norm=True, name="mem_kv")
        mo = memory_attention(q_mem, kv, p["mem_q_gain"][i], p["mem_k_gain"][i], n_seq, seq, n_mem)
        x, xb = matmul([y, mo], [p["b_w_out_y"] if kind == 1 else w_out, w_out], layer=j, out_dtypes=[F32, BF16],
                       row0s=[0, y.shape[1]], res=x, name="out_proj")
        f = matmul([xb], [p["w_ff1"]], layer=i, out_dtypes=[BF16], norm=True, act="relu2", name="ffn_up")
        if i + 1 < depth:
            x, xb = matmul_kred_res(f, p["w_ff2"], x, layer=i, with_bf16=True, name="ffn_down")
        else:
            x = matmul_kred_res(f, p["w_ff2"], x, layer=i, with_bf16=False, name="ffn_down")
    return x.reshape(n_seq, seq, d)


def _fold_gain(w, g):
    return (g.astype(F32)[:, :, None] * w).astype(BF16)


@jax.jit
def kernel(x_prompt, x_sample, mem_prompt, mem_sample, mixer_norm, mem_norm, ffn_norm, mem_w_kv,
           mem_q_gain, mem_k_gain, w_ff1, w_ff2, a_w_in, a_w_out, a_w_s, a_b_s, a_v_gain,
           b_w_in, b_w_out, c_w_in, c_w_out, c_q_gain, c_k_gain):
    depth = mixer_norm.shape[0]
    w_in = [_fold_gain(w, mixer_norm[k::N_MIXERS]) for k, w in enumerate((a_w_in, b_w_in, c_w_in))]
    b_width = b_w_in.shape[2] - MEM_WIDTH
    w_out = [w.astype(BF16) for w in (a_w_out, b_w_out, c_w_out)]
    p = {
        "mem_q_gain": mem_q_gain, "mem_k_gain": mem_k_gain,
        "a_w_s": a_w_s, "a_b_s": a_b_s, "a_v_gain": a_v_gain,
        "c_q_gain": c_q_gain, "c_k_gain": c_k_gain,
        "mem_w_kv": _fold_gain(mem_w_kv, mem_norm),
        "w_ff1": _fold_gain(w_ff1, ffn_norm),
        "w_ff2": w_ff2.astype(BF16),
        "w_in": w_in,
        "b_w_in_dft": jnp.stack([fold_channel_dft(w, b_width) for w in w_in[1]]),
        "w_out": w_out,
        "b_w_out_y": jnp.stack([fourier_row_order(w, b_width) for w in w_out[1]]),
    }
    return _trunk(x_prompt, mem_prompt, p, depth), _trunk(x_sample, mem_sample, p, depth)
```

```python
import functools
import math

import jax
import jax.numpy as jnp
from jax import lax
from jax.experimental import pallas as pl
from jax.experimental.pallas import tpu as pltpu

F32 = jnp.float32
BF16 = jnp.bfloat16

EPS = 1e-6
NEG_INF = -1e30
HEAD_DIM = 128
MEM_HEADS = 4
MEM_WIDTH = MEM_HEADS * HEAD_DIM
A_CHUNK = 128
A_GROUPS = 8
B_GROUPS = 4
C_PAIRS = ((128, 1), (512, 4), (2048, 16))
C_GROUPS = len(C_PAIRS)
C_HEADS = 8
C_WIDTH = C_HEADS * HEAD_DIM
ALIBI_MAX = 8.0
N_MIXERS = 3

V7X_VMEM_BYTES = 64 * 1024 * 1024
VMEM_LIMIT_BYTES = V7X_VMEM_BYTES - 2 * 1024 * 1024
V7X_LANES = 128


def _params(*semantics):
    return pltpu.CompilerParams(dimension_semantics=semantics, vmem_limit_bytes=VMEM_LIMIT_BYTES)


def _pick(n, candidates):
    for c in candidates:
        if n % c == 0:
            return c
    raise ValueError(f"no block size in {candidates} divides {n}")


def _gelu(x):
    return 0.5 * x * (1.0 + lax.erf(x * (1.0 / math.sqrt(2.0))))


def _row_inv_rms(a_ref):
    rows, k = a_ref.shape
    part = jnp.zeros((rows, V7X_LANES), F32)
    for c in range(k // V7X_LANES):
        blk = a_ref[:, c * V7X_LANES:(c + 1) * V7X_LANES].astype(F32)
        part = part + blk * blk
    return lax.rsqrt(jnp.sum(part, axis=-1, keepdims=True) / k + EPS)


def _mm_kernel(*refs, n_ops, norm, act, has_res, n_out, side, cast):
    a_refs = refs[:n_ops]
    w_refs = refs[n_ops:2 * n_ops]
    pos = 2 * n_ops
    ws_ref = refs[pos] if side else None
    pos += side
    res_ref = refs[pos] if has_res else None
    pos += has_res
    cast_in = refs[pos:pos + cast]
    pos += cast
    o_refs = refs[pos:pos + n_out]
    pos += n_out
    os_ref = refs[pos] if side else None
    pos += side
    if cast:
        src = cast_in[0][...]
        refs[pos][...] = (src * cast_in[1][...] if cast == 2 else src).astype(BF16)
        pos += 1
    if norm:
        inv_ref = refs[pos]

        @pl.when(pl.program_id(1) == 0)
        def _():
            inv = _row_inv_rms(a_refs[0])
            inv_ref[...] = jnp.broadcast_to(inv, inv_ref.shape)
            if side:
                d = jnp.dot(a_refs[0][...], ws_ref[...], preferred_element_type=F32)
                os_ref[...] = (d * inv).astype(os_ref.dtype)

    acc = None
    for a_ref, w_ref in zip(a_refs, w_refs):
        d = jnp.dot(a_ref[...], w_ref[...], preferred_element_type=F32)
        acc = d if acc is None else acc + d
    if norm:
        acc = acc * inv_ref[:, 0:1]
    if act == "relu2":
        r = jnp.maximum(acc, 0.0)
        acc = r * r
    elif act == "gelu":
        acc = _gelu(acc)
    if has_res:
        acc = res_ref[...] + acc
    for o_ref in o_refs:
        o_ref[...] = acc.astype(o_ref.dtype)


CAST_TILE_COLS = 1024


def _cast_tiles(steps, k, n):
    ntc = n // CAST_TILE_COLS
    ntr = 1
    while ntr * 2 * ntc <= steps and k % (ntr * 2) == 0:
        ntr *= 2
    assert ntr * ntc <= steps and n % CAST_TILE_COLS == 0
    return ntr, ntc


def _mm_blocks(m, n, k_total, out_bytes, has_res, side_cols, cast_shape):
    for bm, bn in ((1024, 1024), (1024, 512), (512, 1024), (512, 512), (256, 512)):
        if m % bm or n % bn:
            continue
        windows = 2 * (bm * k_total * 2 + k_total * bn * 2 + bm * bn * (out_bytes + 4 * has_res)
                       + (k_total + bm) * side_cols * 2)
        if cast_shape:
            ntr, _ = _cast_tiles((m // bm) * (n // bn), *cast_shape)
            windows += 2 * (cast_shape[0] // ntr) * (CAST_TILE_COLS * 6 + V7X_LANES * 4)
        if windows + 3 * bm * bn * 4 <= VMEM_LIMIT_BYTES:
            return bm, bn
    raise ValueError(f"no matmul blocks fit m={m} n={n} k={k_total}")


def matmul(a_list, w_list, *, out_dtypes, layer=0, n_cols=None, col0=0, row0s=None, norm=False, act=None,
           res=None, side=None, cast=None, name="matmul"):
    m = a_list[0].shape[0]
    n = n_cols if n_cols is not None else w_list[0].shape[2]
    row0s = row0s or [0] * len(a_list)
    ks = [a.shape[1] for a in a_list]
    out_bytes = sum(jnp.dtype(d).itemsize for d in out_dtypes)
    side_cols = side[2] if side else 0
    bm, bn = _mm_blocks(m, n, sum(ks), out_bytes, res is not None, side_cols, cast and cast[0].shape[1:])
    assert col0 % bn == 0 and all(r % k == 0 for r, k in zip(row0s, ks))
    assert not side or (norm and len(a_list) == 1 and side[1] % side[2] == 0)
    cb = col0 // bn
    in_specs = [pl.BlockSpec((bm, k), lambda i, j: (i, 0)) for k in ks]
    in_specs += [pl.BlockSpec((None, k, bn), functools.partial(lambda i, j, rb: (layer, rb, cb + j), rb=r // k))
                 for r, k in zip(row0s, ks)]
    args = list(a_list) + list(w_list)
    out_specs = [pl.BlockSpec((bm, bn), lambda i, j: (i, j))] * len(out_dtypes)
    out_shape = [jax.ShapeDtypeStruct((m, n), d) for d in out_dtypes]
    if side:
        sb = side[1] // side_cols
        in_specs.append(pl.BlockSpec((None, ks[0], side_cols), lambda i, j: (layer, 0, sb)))
        args.append(side[0])
        out_specs.append(pl.BlockSpec((bm, side_cols), lambda i, j: (i, 0)))
        out_shape.append(jax.ShapeDtypeStruct((m, side_cols), BF16))
    if res is not None:
        in_specs.append(pl.BlockSpec((bm, bn), lambda i, j: (i, j)))
        args.append(res)
    n_cast = 0
    if cast:
        src, src_layer, gain = cast
        ck, cn = src.shape[1:]
        nj = n // bn
        ntr, ntc = _cast_tiles((m // bm) * nj, ck, cn)
        tr = ck // ntr

        def tile(i, j):
            t = jnp.minimum(i * nj + j, ntr * ntc - 1)
            return t // ntc, t % ntc

        in_specs.append(pl.BlockSpec((None, tr, CAST_TILE_COLS), lambda i, j: (src_layer, *tile(i, j))))
        args.append(src)
        if gain is not None:
            in_specs.append(pl.BlockSpec((None, tr, 1), lambda i, j: (src_layer, tile(i, j)[0], 0)))
            args.append(gain)
        n_cast = 1 + (gain is not None)
        out_specs = out_specs + [pl.BlockSpec((tr, CAST_TILE_COLS), tile)]
        out_shape = out_shape + [jax.ShapeDtypeStruct((ck, cn), BF16)]
    outs = pl.pallas_call(
        functools.partial(_mm_kernel, n_ops=len(a_list), norm=norm, act=act, has_res=res is not None,
                          n_out=len(out_dtypes), side=bool(side), cast=n_cast),
        grid=(m // bm, n // bn),
        in_specs=in_specs,
        out_specs=out_specs,
        out_shape=out_shape,
        scratch_shapes=[pltpu.VMEM((bm, V7X_LANES), F32)] if norm else [],
        compiler_params=_params("parallel", "arbitrary"),
        name=name,
    )(*args)
    return outs if len(outs) > 1 else outs[0]


def _mm_kred_kernel(a_ref, w_ref, res_ref, o_ref, *maybe_bf16_ref):
    kk = pl.program_id(2)

    @pl.when(kk == 0)
    def _():
        o_ref[...] = res_ref[...]

    o_ref[...] += jnp.dot(a_ref[...], w_ref[...], preferred_element_type=F32)

    for ob_ref in maybe_bf16_ref:
        @pl.when(kk == pl.num_programs(2) - 1)
        def _():
            ob_ref[...] = o_ref[...].astype(ob_ref.dtype)


def matmul_kred_res(a, w, res, *, layer, with_bf16, name="matmul_kred"):
    m, k = a.shape
    n = w.shape[2]
    bm = _pick(m, (1024, 512, 256))
    bn = _pick(n, (1024, 512))
    bk = _pick(k, (4096, 2048, 1024))
    out_spec = pl.BlockSpec((bm, bn), lambda i, j, kk: (i, j))
    out_shape = [jax.ShapeDtypeStruct((m, n), F32)]
    if with_bf16:
        out_shape.append(jax.ShapeDtypeStruct((m, n), BF16))
    outs = pl.pallas_call(
        _mm_kred_kernel,
        grid=(m // bm, n // bn, k // bk),
        in_specs=[pl.BlockSpec((bm, bk), lambda i, j, kk: (i, kk)),
                  pl.BlockSpec((None, bk, bn), lambda i, j, kk: (layer, kk, j)),
                  pl.BlockSpec((bm, bn), lambda i, j, kk: (i, j))],
        out_specs=[out_spec] * len(out_shape),
        out_shape=out_shape,
        compiler_params=_params("parallel", "parallel", "arbitrary"),
        name=name,
    )(a, w, res)
    return outs if with_bf16 else outs[0]


def _gate_kernel(u_ref, v_ref, vg_ref, ws_ref, bs_ref, o_ref, *, n_chunks, gw):
    inv = _row_inv_rms(v_ref)

    def gate(g, carry):
        c0 = pl.multiple_of(g * gw, gw)
        vn = ((v_ref[:, pl.ds(c0, gw)].astype(F32) * inv) * vg_ref[:, pl.ds(c0, gw)]).astype(BF16)
        u = u_ref[:, pl.ds(c0, gw)].astype(F32)
        w = ws_ref[g]
        b = bs_ref[g]
        for c in range(n_chunks):
            r0 = c * A_CHUNK
            vm = jnp.dot(w, vn[r0:r0 + A_CHUNK], preferred_element_type=F32) + b
            o_ref[r0:r0 + A_CHUNK, pl.ds(c0, gw)] = (u[r0:r0 + A_CHUNK] * vm).astype(o_ref.dtype)
        return carry

    lax.fori_loop(0, A_GROUPS, gate, 0)


def gated_chunk_mlp(uv, w_s, b_s, v_gain):
    m, w2 = uv.shape
    width = w2 // 2
    gw = width // A_GROUPS
    rows = _pick(m, (512, 256, 128))
    return pl.pallas_call(
        functools.partial(_gate_kernel, n_chunks=rows // A_CHUNK, gw=gw),
        grid=(m // rows,),
        in_specs=[pl.BlockSpec((rows, width), lambda i: (i, 0)),
                  pl.BlockSpec((rows, width), lambda i: (i, 1)),
                  pl.BlockSpec((1, width), lambda i: (0, 0)),
                  pl.BlockSpec((A_GROUPS, A_CHUNK, A_CHUNK), lambda i: (0, 0, 0)),
                  pl.BlockSpec((A_GROUPS, A_CHUNK, 1), lambda i: (0, 0, 0))],
        out_specs=pl.BlockSpec((rows, width), lambda i: (i, 0)),
        out_shape=jax.ShapeDtypeStruct((m, width), BF16),
        compiler_params=_params("parallel"),
        name="gated_chunk_mlp",
    )(uv, uv, v_gain.reshape(1, width).astype(F32), w_s.astype(BF16),
      b_s.reshape(A_GROUPS, A_CHUNK, 1).astype(F32))


def _dft_tables(n):
    idx = lax.iota(jnp.int32, n)
    prod = (idx[:, None] * idx[None, :]) % n
    theta = prod.astype(F32) * (2.0 * math.pi / n)
    scale = 1.0 / math.sqrt(n)
    return jnp.cos(theta) * scale, -jnp.sin(theta) * scale


def _half_channel_dft(gc):
    cos, nsin = _dft_tables(gc)
    half = gc // 2
    return jnp.concatenate([cos[:, :half + 1], -nsin[:, 1:half]], axis=1)


def _fourier_channel_order(gc):
    half = gc // 2
    return list(range(half + 1)) + list(range(gc - 1, half, -1))


def fourier_row_order(w_out, width):
    n = w_out.shape[1]
    gc = width // B_GROUPS
    order = jnp.asarray(_fourier_channel_order(gc), jnp.int32)
    onehot = (order[:, None] == lax.iota(jnp.int32, gc)[None, :]).astype(BF16)
    bn = _pick(n, (1024, 512))
    return pl.pallas_call(
        _chan_dft_kernel,
        grid=(B_GROUPS, n // bn),
        in_specs=[pl.BlockSpec((gc, gc), lambda g, j: (0, 0)),
                  pl.BlockSpec((gc, bn), lambda g, j: (g, j))],
        out_specs=pl.BlockSpec((gc, bn), lambda g, j: (g, j)),
        out_shape=jax.ShapeDtypeStruct((width, n), BF16),
        compiler_params=_params("parallel", "parallel"),
        name="fourier_row_order",
    )(onehot, w_out)


def _chan_dft_kernel(z_ref, w_ref, o_ref):
    o_ref[...] = jnp.dot(z_ref[...], w_ref[...], preferred_element_type=F32).astype(o_ref.dtype)


def fold_channel_dft(w_in, width):
    k = w_in.shape[0]
    gc = width // B_GROUPS
    bm = _pick(k, (1024, 512, 256))
    return pl.pallas_call(
        _chan_dft_kernel,
        grid=(k // bm, B_GROUPS),
        in_specs=[pl.BlockSpec((bm, gc), lambda i, g: (i, g)),
                  pl.BlockSpec((gc, gc), lambda i, g: (0, 0))],
        out_specs=pl.BlockSpec((bm, gc), lambda i, g: (i, g)),
        out_shape=jax.ShapeDtypeStruct((k, width), BF16),
        compiler_params=_params("parallel", "parallel"),
        name="fold_channel_dft",
    )(w_in, _half_channel_dft(gc).astype(BF16))


def _pos_dft_kernel(c_ref, ns_ref, pc_ref, pq_ref, o_ref):
    half = pc_ref.shape[1]
    a = jnp.dot(c_ref[...], pc_ref[...], preferred_element_type=F32)
    b = jnp.dot(ns_ref[...], pq_ref[...], preferred_element_type=F32)
    nyq = jnp.dot(c_ref[...], pq_ref[:, :V7X_LANES], preferred_element_type=F32)[:, 0:1]
    first = lax.broadcasted_iota(jnp.int32, a.shape, 1) == 0
    o_ref[:, :half] = (a + jnp.where(first, 0.0, b)).astype(o_ref.dtype)
    o_ref[:, half:] = jnp.where(first, nyq, a - b).astype(o_ref.dtype)


def fourier_positions(p, n_seq, seq):
    m, width = p.shape
    gc = width // B_GROUPS
    half = gc // 2
    cos, nsin = _dft_tables(seq)
    bo = _pick(seq, (1024, 512, 256))
    nb = seq // bo
    return pl.pallas_call(
        _pos_dft_kernel,
        grid=(n_seq, B_GROUPS, nb),
        in_specs=[pl.BlockSpec((bo, seq), lambda b, g, i: (i, 0)),
                  pl.BlockSpec((bo, seq), lambda b, g, i: (i, 0)),
                  pl.BlockSpec((seq, half), lambda b, g, i: (b, 2 * g)),
                  pl.BlockSpec((seq, half), lambda b, g, i: (b, 2 * g + 1))],
        out_specs=pl.BlockSpec((bo, gc), lambda b, g, i: (b * nb + i, g)),
        out_shape=jax.ShapeDtypeStruct((m, width), BF16),
        compiler_params=_params("parallel", "parallel", "arbitrary"),
        name="fourier_positions",
    )(cos.astype(BF16), nsin.astype(BF16), p, p)


def _head_norm(x, g):
    x = x.astype(F32)
    ms = jnp.mean(x * x, axis=-1, keepdims=True)
    return (x * lax.rsqrt(ms + EPS)) * g


def _band_bias(n_blocks, qblk, radius, dilation, slope, length):
    kw = qblk + 2 * radius
    row = lax.broadcasted_iota(jnp.int32, (qblk, kw), 0)
    col = lax.broadcasted_iota(jnp.int32, (qblk, kw), 1)
    dist = jnp.abs(row - (col - radius))
    inner = jnp.where(dist <= radius, -slope * (dist * dilation).astype(F32), NEG_INF)
    first = jnp.where(col < radius, NEG_INF, inner)
    last_col = length - (n_blocks - 1) * qblk + radius
    last = jnp.where(col >= last_col, NEG_INF, inner)
    if n_blocks == 1:
        return jnp.where(col >= last_col, NEG_INF, first)[None]
    return jnp.stack([first] + [inner] * (n_blocks - 2) + [last])


def _windows(x, n_blocks, qblk, radius):
    pad = jnp.zeros((radius, x.shape[1]), x.dtype)
    xp = jnp.concatenate([pad, x, pad], axis=0)
    return jnp.stack([xp[t * qblk:t * qblk + qblk + 2 * radius] for t in range(n_blocks)])


def _block_softmax_pv(q, k, v, bias):
    scores = jnp.einsum("bqc,bkc->bqk", q, k, preferred_element_type=F32) * (HEAD_DIM ** -0.5)
    logits = scores + bias
    mx = jnp.max(logits, axis=-1, keepdims=True)
    p = jnp.exp(logits - mx)
    den = jnp.sum(p, axis=-1, keepdims=True)
    pv = jnp.einsum("bqk,bkc->bqc", p.astype(BF16), v, preferred_element_type=F32)
    return mx, den, pv


def _dil_attn_kernel(*refs, seq):
    qkv_refs = refs[:3 * C_GROUPS]
    qg_ref, kg_ref, sl_ref, o_ref, q_s, k_s, v_s, m_s, l_s, acc_s = refs[3 * C_GROUPS:]
    slope = sl_ref[:, 0:1]
    lanes = (HEAD_DIM,)
    for g, (window, dilation) in enumerate(C_PAIRS):
        radius = window // (2 * dilation)
        length = seq // dilation
        qblk = min(length, 128)
        nb = length // qblk
        q_ref, k_ref, v_ref = qkv_refs[3 * g:3 * g + 3]
        qn = _head_norm(q_ref[...], qg_ref[g:g + 1, :])
        kn = _head_norm(k_ref[...], kg_ref[g:g + 1, :])

        def residue(ref, r):
            return ref[pl.ds(r, length, stride=dilation), :]

        if dilation == 1:
            qr, kr, vr = [qn.astype(BF16)], [kn.astype(BF16)], [v_ref[...]]
        else:
            q_s[...] = qn
            k_s[...] = kn
            v_s[...] = v_ref[...].astype(F32)
            qr = [residue(q_s, r).astype(BF16) for r in range(dilation)]
            kr = [residue(k_s, r).astype(BF16) for r in range(dilation)]
            vr = [residue(v_s, r).astype(BF16) for r in range(dilation)]
        qb = jnp.concatenate([x.reshape(nb, qblk, HEAD_DIM) for x in qr])
        kb = jnp.concatenate([_windows(x, nb, qblk, radius) for x in kr])
        vb = jnp.concatenate([_windows(x, nb, qblk, radius) for x in vr])
        bias = _band_bias(nb, qblk, radius, dilation, slope, length)
        bias = jnp.concatenate([bias] * dilation)
        mx, den, pv = _block_softmax_pv(qb, kb, vb, bias)
        mx = jnp.broadcast_to(mx, mx.shape[:2] + lanes)
        den = jnp.broadcast_to(den, den.shape[:2] + lanes)
        if g == 0:
            m_s[...] = mx.reshape(seq, HEAD_DIM)
            l_s[...] = den.reshape(seq, HEAD_DIM)
            acc_s[...] = pv.reshape(seq, HEAD_DIM)
            continue
        for r in range(dilation):
            rows = slice(r * nb, (r + 1) * nb)
            m_old = residue(m_s, r)
            m_blk = mx[rows].reshape(length, HEAD_DIM)
            m_new = jnp.maximum(m_old, m_blk)
            a = jnp.exp(m_old - m_new)
            b = jnp.exp(m_blk - m_new)
            l_new = a * residue(l_s, r) + b * den[rows].reshape(length, HEAD_DIM)
            acc_new = a * residue(acc_s, r) + b * pv[rows].reshape(length, HEAD_DIM)
            m_s[pl.ds(r, length, stride=dilation), :] = m_new
            l_s[pl.ds(r, length, stride=dilation), :] = l_new
            acc_s[pl.ds(r, length, stride=dilation), :] = acc_new
    o_ref[...] = (acc_s[...] / l_s[...]).astype(o_ref.dtype)


def dilated_mixture(z, q_gain, k_gain, n_seq, seq):
    for window, dilation in C_PAIRS:
        assert seq % (dilation * (window // (2 * dilation))) == 0
    slopes = 2.0 ** (-ALIBI_MAX * jnp.arange(1, C_HEADS + 1, dtype=F32) / C_HEADS)
    slopes = jnp.repeat(slopes, HEAD_DIM).reshape(1, C_WIDTH)
    heads_per_part = C_WIDTH // HEAD_DIM

    def qkv_spec(g, part):
        col = (g * 3 + part) * heads_per_part
        return pl.BlockSpec((seq, HEAD_DIM), lambda b, h: (b, col + h))

    gain_spec = pl.BlockSpec((C_GROUPS, HEAD_DIM), lambda b, h: (0, 0))
    state = pltpu.VMEM((seq, HEAD_DIM), F32)
    return pl.pallas_call(
        functools.partial(_dil_attn_kernel, seq=seq),
        grid=(n_seq, C_HEADS),
        in_specs=[qkv_spec(g, part) for g in range(C_GROUPS) for part in range(3)]
                 + [gain_spec, gain_spec, pl.BlockSpec((1, HEAD_DIM), lambda b, h: (0, h))],
        out_specs=pl.BlockSpec((seq, HEAD_DIM), lambda b, h: (b, h)),
        out_shape=jax.ShapeDtypeStruct((n_seq * seq, C_WIDTH), BF16),
        scratch_shapes=[state] * 6,
        compiler_params=_params("parallel", "parallel"),
        name="dilated_attention",
    )(*([z] * (3 * C_GROUPS)), q_gain.astype(F32), k_gain.astype(F32), slopes)


def _mem_attn_kernel(q_ref, kv_ref, qg_ref, kg_ref, o_ref):
    scale = HEAD_DIM ** -0.5
    for h in range(MEM_HEADS):
        c0 = h * HEAD_DIM
        qn = _head_norm(q_ref[:, c0:c0 + HEAD_DIM], qg_ref[...]).astype(BF16)
        kn = _head_norm(kv_ref[:, c0:c0 + HEAD_DIM], kg_ref[...]).astype(BF16)
        v = kv_ref[:, MEM_WIDTH + c0:MEM_WIDTH + c0 + HEAD_DIM].astype(BF16)
        s = lax.dot_general(qn, kn, (((1,), (1,)), ((), ())), preferred_element_type=F32) * scale
        mx = jnp.max(s, axis=-1, keepdims=True)
        e = jnp.exp(s - mx)
        p = e / jnp.sum(e, axis=-1, keepdims=True)
        o_ref[:, c0:c0 + HEAD_DIM] = jnp.dot(p.astype(BF16), v, preferred_element_type=F32).astype(o_ref.dtype)


def memory_attention(q, kv, q_gain, k_gain, n_seq, seq, n_mem):
    bm = _pick(seq, (1024, 512, 256))
    nb = seq // bm
    gain_spec = pl.BlockSpec((1, HEAD_DIM), lambda b, i: (0, 0))
    return pl.pallas_call(
        _mem_attn_kernel,
        grid=(n_seq, nb),
        in_specs=[pl.BlockSpec((bm, MEM_WIDTH), lambda b, i: (b * nb + i, 0)),
                  pl.BlockSpec((n_mem, 2 * MEM_WIDTH), lambda b, i: (b, 0)),
                  gain_spec, gain_spec],
        out_specs=pl.BlockSpec((bm, MEM_WIDTH), lambda b, i: (b * nb + i, 0)),
        out_shape=jax.ShapeDtypeStruct((n_seq * seq, MEM_WIDTH), BF16),
        compiler_params=_params("parallel", "arbitrary"),
        name="memory_attention",
    )(q, kv, q_gain.reshape(1, HEAD_DIM).astype(F32), k_gain.reshape(1, HEAD_DIM).astype(F32))


def _trunk(x, mem, p, depth, ffn_w=None):
    n_seq, seq, d = x.shape
    n_mem = mem.shape[1]
    x = x.reshape(n_seq * seq, d)
    xb = x.astype(BF16)
    memb = mem.reshape(n_seq * n_mem, d).astype(BF16)
    convert = ffn_w is None
    ffn_w = [] if convert else ffn_w
    for i in range(depth):
        kind, j = i % N_MIXERS, i // N_MIXERS
        w_in, w_out = p["w_in"][kind], p["w_out"][kind]
        y_width = w_in.shape[2] - MEM_WIDTH
        z, q_mem = matmul([xb], [p["b_w_in_dft"] if kind == 1 else w_in], layer=j, out_dtypes=[BF16],
                          n_cols=y_width, norm=True, act="gelu" if kind == 0 else None,
                          side=(w_in, y_width, MEM_WIDTH), name="in_proj")
        if kind == 0:
            y = gated_chunk_mlp(z, p["a_w_s"][j], p["a_b_s"][j], p["a_v_gain"][j])
        elif kind == 1:
            y = fourier_positions(z, n_seq, seq)
        else:
            y = dilated_mixture(z, p["c_q_gain"][j], p["c_k_gain"][j], n_seq, seq)
        kv = matmul([memb], [p["mem_w_kv"]], layer=i, out_dtypes=[F32], norm=True, name="mem_kv")
        mo = memory_attention(q_mem, kv, p["mem_q_gain"][i], p["mem_k_gain"][i], n_seq, seq, n_mem)
        outs = matmul([y, mo], [p["b_w_out_y"] if kind == 1 else w_out, w_out], layer=j, out_dtypes=[F32, BF16],
                      row0s=[0, y.shape[1]], res=x, cast=(p["w_ff1"], i, p["ffn_gain"]) if convert else None,
                      name="out_proj")
        x, xb = outs[:2]
        w1 = outs[2] if convert else ffn_w[i][0]
        outs = matmul([xb], [w1[None]], out_dtypes=[BF16], norm=True, act="relu2",
                      cast=(p["w_ff2"], i, None) if convert else None, name="ffn_up")
        f, w2 = outs if convert else (outs, ffn_w[i][1])
        if convert:
            ffn_w.append((w1, w2))
        if i + 1 < depth:
            x, xb = matmul_kred_res(f, w2[None], x, layer=0, with_bf16=True, name="ffn_down")
        else:
            x = matmul_kred_res(f, w2[None], x, layer=0, with_bf16=False, name="ffn_down")
    return x.reshape(n_seq, seq, d), ffn_w


def _fold_gain(w, g):
    return (g.astype(F32)[:, :, None] * w).astype(BF16)


@jax.jit
def kernel(x_prompt, x_sample, mem_prompt, mem_sample, mixer_norm, mem_norm, ffn_norm, mem_w_kv,
           mem_q_gain, mem_k_gain, w_ff1, w_ff2, a_w_in, a_w_out, a_w_s, a_b_s, a_v_gain,
           b_w_in, b_w_out, c_w_in, c_w_out, c_q_gain, c_k_gain):
    depth = mixer_norm.shape[0]
    w_in = [_fold_gain(w, mixer_norm[k::N_MIXERS]) for k, w in enumerate((a_w_in, b_w_in, c_w_in))]
    b_width = b_w_in.shape[2] - MEM_WIDTH
    w_out = [w.astype(BF16) for w in (a_w_out, b_w_out, c_w_out)]
    p = {
        "mem_q_gain": mem_q_gain, "mem_k_gain": mem_k_gain,
        "a_w_s": a_w_s, "a_b_s": a_b_s, "a_v_gain": a_v_gain,
        "c_q_gain": c_q_gain, "c_k_gain": c_k_gain,
        "mem_w_kv": _fold_gain(mem_w_kv, mem_norm),
        "w_ff1": w_ff1, "w_ff2": w_ff2, "ffn_gain": ffn_norm.astype(F32)[:, :, None],
        "w_in": w_in,
        "b_w_in_dft": jnp.stack([fold_channel_dft(w, b_width) for w in w_in[1]]),
        "w_out": w_out,
        "b_w_out_y": jnp.stack([fourier_row_order(w, b_width) for w in w_out[1]]),
    }
    y_prompt, ffn_w = _trunk(x_prompt, mem_prompt, p, depth)
    y_sample, _ = _trunk(x_sample, mem_sample, p, depth, ffn_w)
    return y_prompt, y_sample
```

```python
import functools
import math

import jax
import jax.numpy as jnp
from jax import lax
from jax.experimental import pallas as pl
from jax.experimental.pallas import tpu as pltpu

F32 = jnp.float32
BF16 = jnp.bfloat16

EPS = 1e-6
NEG_INF = -1e30
HEAD_DIM = 128
MEM_HEADS = 4
MEM_WIDTH = MEM_HEADS * HEAD_DIM
A_CHUNK = 128
A_GROUPS = 8
B_GROUPS = 4
C_PAIRS = ((128, 1), (512, 4), (2048, 16))
C_GROUPS = len(C_PAIRS)
C_HEADS = 8
C_WIDTH = C_HEADS * HEAD_DIM
ALIBI_MAX = 8.0
N_MIXERS = 3

V7X_VMEM_BYTES = 64 * 1024 * 1024
VMEM_LIMIT_BYTES = V7X_VMEM_BYTES - 2 * 1024 * 1024
V7X_LANES = 128


def _params(*semantics):
    return pltpu.CompilerParams(dimension_semantics=semantics, vmem_limit_bytes=VMEM_LIMIT_BYTES)


def _pick(n, candidates):
    for c in candidates:
        if n % c == 0:
            return c
    raise ValueError(f"no block size in {candidates} divides {n}")


def _gelu(x):
    return 0.5 * x * (1.0 + lax.erf(x * (1.0 / math.sqrt(2.0))))


def _row_inv_rms(a_ref):
    rows, k = a_ref.shape
    part = jnp.zeros((rows, V7X_LANES), F32)
    for c in range(k // V7X_LANES):
        blk = a_ref[:, c * V7X_LANES:(c + 1) * V7X_LANES].astype(F32)
        part = part + blk * blk
    return lax.rsqrt(jnp.sum(part, axis=-1, keepdims=True) / k + EPS)


def _mm_kernel(*refs, n_ops, norm, act, has_res, n_out, side, cast):
    a_refs = refs[:n_ops]
    w_refs = refs[n_ops:2 * n_ops]
    pos = 2 * n_ops
    ws_ref = refs[pos] if side else None
    pos += side
    res_ref = refs[pos] if has_res else None
    pos += has_res
    cast_in = refs[pos:pos + cast]
    pos += cast
    o_refs = refs[pos:pos + n_out]
    pos += n_out
    os_ref = refs[pos] if side else None
    pos += side
    if cast:
        src = cast_in[0][...]
        refs[pos][...] = (src * cast_in[1][...] if cast == 2 else src).astype(BF16)
        pos += 1
    if norm:
        inv_ref = refs[pos]

        @pl.when(pl.program_id(1) == 0)
        def _():
            inv = _row_inv_rms(a_refs[0])
            inv_ref[...] = jnp.broadcast_to(inv, inv_ref.shape)
            if side:
                d = jnp.dot(a_refs[0][...], ws_ref[...], preferred_element_type=F32)
                os_ref[...] = (d * inv).astype(os_ref.dtype)

    acc = None
    for a_ref, w_ref in zip(a_refs, w_refs):
        d = jnp.dot(a_ref[...], w_ref[...], preferred_element_type=F32)
        acc = d if acc is None else acc + d
    if norm:
        acc = acc * inv_ref[:, 0:1]
    if act == "relu2":
        r = jnp.maximum(acc, 0.0)
        acc = r * r
    elif act == "gelu":
        acc = _gelu(acc)
    if has_res:
        acc = res_ref[...] + acc
    for o_ref in o_refs:
        o_ref[...] = acc.astype(o_ref.dtype)


CAST_TILE_COLS = 1024


def _cast_tiles(steps, k, n):
    ntc = n // CAST_TILE_COLS
    ntr = 1
    while ntr * 2 * ntc <= steps and k % (ntr * 2) == 0:
        ntr *= 2
    assert ntr * ntc <= steps and n % CAST_TILE_COLS == 0
    return ntr, ntc


def _mm_blocks(m, n, k_total, out_bytes, has_res, side_cols, cast_shape):
    for bm, bn in ((1024, 1024), (1024, 512), (512, 1024), (512, 512), (256, 512)):
        if m % bm or n % bn:
            continue
        windows = 2 * (bm * k_total * 2 + k_total * bn * 2 + bm * bn * (out_bytes + 4 * has_res)
                       + (k_total + bm) * side_cols * 2)
        if cast_shape:
            ntr, _ = _cast_tiles((m // bm) * (n // bn), *cast_shape)
            windows += 2 * (cast_shape[0] // ntr) * (CAST_TILE_COLS * 6 + V7X_LANES * 4)
        if windows + 3 * bm * bn * 4 <= VMEM_LIMIT_BYTES:
            return bm, bn
    raise ValueError(f"no matmul blocks fit m={m} n={n} k={k_total}")


def matmul(a_list, w_list, *, out_dtypes, layer=0, n_cols=None, col0=0, row0s=None, norm=False, act=None,
           res=None, side=None, cast=None, name="matmul"):
    m = a_list[0].shape[0]
    n = n_cols if n_cols is not None else w_list[0].shape[2]
    row0s = row0s or [0] * len(a_list)
    ks = [a.shape[1] for a in a_list]
    out_bytes = sum(jnp.dtype(d).itemsize for d in out_dtypes)
    side_cols = side[2] if side else 0
    bm, bn = _mm_blocks(m, n, sum(ks), out_bytes, res is not None, side_cols, cast and cast[0].shape[1:])
    assert col0 % bn == 0 and all(r % k == 0 for r, k in zip(row0s, ks))
    assert not side or (norm and len(a_list) == 1 and side[1] % side[2] == 0)
    cb = col0 // bn
    in_specs = [pl.BlockSpec((bm, k), lambda i, j: (i, 0)) for k in ks]
    in_specs += [pl.BlockSpec((None, k, bn), functools.partial(lambda i, j, rb: (layer, rb, cb + j), rb=r // k))
                 for r, k in zip(row0s, ks)]
    args = list(a_list) + list(w_list)
    out_specs = [pl.BlockSpec((bm, bn), lambda i, j: (i, j))] * len(out_dtypes)
    out_shape = [jax.ShapeDtypeStruct((m, n), d) for d in out_dtypes]
    if side:
        sb = side[1] // side_cols
        in_specs.append(pl.BlockSpec((None, ks[0], side_cols), lambda i, j: (layer, 0, sb)))
        args.append(side[0])
        out_specs.append(pl.BlockSpec((bm, side_cols), lambda i, j: (i, 0)))
        out_shape.append(jax.ShapeDtypeStruct((m, side_cols), BF16))
    if res is not None:
        in_specs.append(pl.BlockSpec((bm, bn), lambda i, j: (i, j)))
        args.append(res)
    n_cast = 0
    if cast:
        src, src_layer, gain = cast
        ck, cn = src.shape[1:]
        nj = n // bn
        ntr, ntc = _cast_tiles((m // bm) * nj, ck, cn)
        tr = ck // ntr

        def tile(i, j):
            t = jnp.minimum(i * nj + j, ntr * ntc - 1)
            return t // ntc, t % ntc

        in_specs.append(pl.BlockSpec((None, tr, CAST_TILE_COLS), lambda i, j: (src_layer, *tile(i, j))))
        args.append(src)
        if gain is not None:
            in_specs.append(pl.BlockSpec((None, tr, 1), lambda i, j: (src_layer, tile(i, j)[0], 0)))
            args.append(gain)
        n_cast = 1 + (gain is not None)
        out_specs = out_specs + [pl.BlockSpec((tr, CAST_TILE_COLS), tile)]
        out_shape = out_shape + [jax.ShapeDtypeStruct((ck, cn), BF16)]
    outs = pl.pallas_call(
        functools.partial(_mm_kernel, n_ops=len(a_list), norm=norm, act=act, has_res=res is not None,
                          n_out=len(out_dtypes), side=bool(side), cast=n_cast),
        grid=(m // bm, n // bn),
        in_specs=in_specs,
        out_specs=out_specs,
        out_shape=out_shape,
        scratch_shapes=[pltpu.VMEM((bm, V7X_LANES), F32)] if norm else [],
        compiler_params=_params("parallel", "arbitrary"),
        name=name,
    )(*args)
    return outs if len(outs) > 1 else outs[0]


def _mm_kred_kernel(a_ref, w_ref, res_ref, o_ref, *maybe_bf16_ref):
    kk = pl.program_id(2)

    def dot():
        return jnp.dot(a_ref[...], w_ref[...], preferred_element_type=F32)

    @pl.when(kk == 0)
    def _():
        o_ref[...] = res_ref[...] + dot()

    @pl.when(kk > 0)
    def _():
        o_ref[...] += dot()

    for ob_ref in maybe_bf16_ref:
        @pl.when(kk == pl.num_programs(2) - 1)
        def _():
            ob_ref[...] = o_ref[...].astype(ob_ref.dtype)


def matmul_kred_res(a, w, res, *, layer, with_bf16, name="matmul_kred"):
    m, k = a.shape
    n = w.shape[2]
    bm = _pick(m, (1024, 512, 256))
    bn = _pick(n, (1024, 512))
    bk = _pick(k, (4096, 2048, 1024))
    out_spec = pl.BlockSpec((bm, bn), lambda i, j, kk: (i, j))
    out_shape = [jax.ShapeDtypeStruct((m, n), F32)]
    if with_bf16:
        out_shape.append(jax.ShapeDtypeStruct((m, n), BF16))
    outs = pl.pallas_call(
        _mm_kred_kernel,
        grid=(m // bm, n // bn, k // bk),
        in_specs=[pl.BlockSpec((bm, bk), lambda i, j, kk: (i, kk)),
                  pl.BlockSpec((None, bk, bn), lambda i, j, kk: (layer, kk, j)),
                  pl.BlockSpec((bm, bn), lambda i, j, kk: (i, j))],
        out_specs=[out_spec] * len(out_shape),
        out_shape=out_shape,
        compiler_params=_params("parallel", "parallel", "arbitrary"),
        name=name,
    )(a, w, res)
    return outs if with_bf16 else outs[0]


def _gate_kernel(u_ref, v_ref, vg_ref, ws_ref, bs_ref, o_ref, *, n_chunks, gw):
    inv = _row_inv_rms(v_ref)

    def gate(g, carry):
        c0 = pl.multiple_of(g * gw, gw)
        vn = ((v_ref[:, pl.ds(c0, gw)].astype(F32) * inv) * vg_ref[:, pl.ds(c0, gw)]).astype(BF16)
        u = u_ref[:, pl.ds(c0, gw)].astype(F32)
        w = ws_ref[g]
        b = bs_ref[g]
        for c in range(n_chunks):
            r0 = c * A_CHUNK
            vm = jnp.dot(w, vn[r0:r0 + A_CHUNK], preferred_element_type=F32) + b
            o_ref[r0:r0 + A_CHUNK, pl.ds(c0, gw)] = (u[r0:r0 + A_CHUNK] * vm).astype(o_ref.dtype)
        return carry

    lax.fori_loop(0, A_GROUPS, gate, 0)


def gated_chunk_mlp(uv, w_s, b_s, v_gain):
    m, w2 = uv.shape
    width = w2 // 2
    gw = width // A_GROUPS
    rows = _pick(m, (512, 256, 128))
    return pl.pallas_call(
        functools.partial(_gate_kernel, n_chunks=rows // A_CHUNK, gw=gw),
        grid=(m // rows,),
        in_specs=[pl.BlockSpec((rows, width), lambda i: (i, 0)),
                  pl.BlockSpec((rows, width), lambda i: (i, 1)),
                  pl.BlockSpec((1, width), lambda i: (0, 0)),
                  pl.BlockSpec((A_GROUPS, A_CHUNK, A_CHUNK), lambda i: (0, 0, 0)),
                  pl.BlockSpec((A_GROUPS, A_CHUNK, 1), lambda i: (0, 0, 0))],
        out_specs=pl.BlockSpec((rows, width), lambda i: (i, 0)),
        out_shape=jax.ShapeDtypeStruct((m, width), BF16),
        compiler_params=_params("parallel"),
        name="gated_chunk_mlp",
    )(uv, uv, v_gain.reshape(1, width).astype(F32), w_s.astype(BF16),
      b_s.reshape(A_GROUPS, A_CHUNK, 1).astype(F32))


def _dft_tables(n):
    idx = lax.iota(jnp.int32, n)
    prod = (idx[:, None] * idx[None, :]) % n
    theta = prod.astype(F32) * (2.0 * math.pi / n)
    scale = 1.0 / math.sqrt(n)
    return jnp.cos(theta) * scale, -jnp.sin(theta) * scale


def _half_channel_dft(gc):
    cos, nsin = _dft_tables(gc)
    half = gc // 2
    return jnp.concatenate([cos[:, :half + 1], -nsin[:, 1:half]], axis=1)


def _fourier_channel_order(gc):
    half = gc // 2
    return list(range(half + 1)) + list(range(gc - 1, half, -1))


def fourier_row_order(w_out, width):
    n = w_out.shape[1]
    gc = width // B_GROUPS
    order = jnp.asarray(_fourier_channel_order(gc), jnp.int32)
    onehot = (order[:, None] == lax.iota(jnp.int32, gc)[None, :]).astype(BF16)
    bn = _pick(n, (1024, 512))
    return pl.pallas_call(
        _chan_dft_kernel,
        grid=(B_GROUPS, n // bn),
        in_specs=[pl.BlockSpec((gc, gc), lambda g, j: (0, 0)),
                  pl.BlockSpec((gc, bn), lambda g, j: (g, j))],
        out_specs=pl.BlockSpec((gc, bn), lambda g, j: (g, j)),
        out_shape=jax.ShapeDtypeStruct((width, n), BF16),
        compiler_params=_params("parallel", "parallel"),
        name="fourier_row_order",
    )(onehot, w_out)


def _chan_dft_kernel(z_ref, w_ref, o_ref):
    o_ref[...] = jnp.dot(z_ref[...], w_ref[...], preferred_element_type=F32).astype(o_ref.dtype)


def fold_channel_dft(w_in, width):
    k = w_in.shape[0]
    gc = width // B_GROUPS
    bm = _pick(k, (1024, 512, 256))
    return pl.pallas_call(
        _chan_dft_kernel,
        grid=(k // bm, B_GROUPS),
        in_specs=[pl.BlockSpec((bm, gc), lambda i, g: (i, g)),
                  pl.BlockSpec((gc, gc), lambda i, g: (0, 0))],
        out_specs=pl.BlockSpec((bm, gc), lambda i, g: (i, g)),
        out_shape=jax.ShapeDtypeStruct((k, width), BF16),
        compiler_params=_params("parallel", "parallel"),
        name="fold_channel_dft",
    )(w_in, _half_channel_dft(gc).astype(BF16))


def _pos_dft_kernel(c_ref, ns_ref, pc_ref, pq_ref, o_ref):
    half = pc_ref.shape[1]
    a = jnp.dot(c_ref[...], pc_ref[...], preferred_element_type=F32)
    b = jnp.dot(ns_ref[...], pq_ref[...], preferred_element_type=F32)
    nyq = jnp.dot(c_ref[...], pq_ref[:, :V7X_LANES], preferred_element_type=F32)[:, 0:1]
    first = lax.broadcasted_iota(jnp.int32, a.shape, 1) == 0
    o_ref[:, :half] = (a + jnp.where(first, 0.0, b)).astype(o_ref.dtype)
    o_ref[:, half:] = jnp.where(first, nyq, a - b).astype(o_ref.dtype)


def fourier_positions(p, n_seq, seq):
    m, width = p.shape
    gc = width // B_GROUPS
    half = gc // 2
    cos, nsin = _dft_tables(seq)
    bo = _pick(seq, (1024, 512, 256))
    nb = seq // bo
    return pl.pallas_call(
        _pos_dft_kernel,
        grid=(n_seq, B_GROUPS, nb),
        in_specs=[pl.BlockSpec((bo, seq), lambda b, g, i: (i, 0)),
                  pl.BlockSpec((bo, seq), lambda b, g, i: (i, 0)),
                  pl.BlockSpec((seq, half), lambda b, g, i: (b, 2 * g)),
                  pl.BlockSpec((seq, half), lambda b, g, i: (b, 2 * g + 1))],
        out_specs=pl.BlockSpec((bo, gc), lambda b, g, i: (b * nb + i, g)),
        out_shape=jax.ShapeDtypeStruct((m, width), BF16),
        compiler_params=_params("parallel", "parallel", "arbitrary"),
        name="fourier_positions",
    )(cos.astype(BF16), nsin.astype(BF16), p, p)


def _head_norm(x, g):
    x = x.astype(F32)
    ms = jnp.mean(x * x, axis=-1, keepdims=True)
    return (x * lax.rsqrt(ms + EPS)) * g


def _key_pad(n_blocks, radius):
    return 0 if n_blocks == 1 else radius


def _band_bias(n_blocks, qblk, radius, dilation, slope, length):
    pad = _key_pad(n_blocks, radius)
    kw = qblk + 2 * pad
    row = lax.broadcasted_iota(jnp.int32, (qblk, kw), 0)
    col = lax.broadcasted_iota(jnp.int32, (qblk, kw), 1)
    dist = jnp.abs(row - (col - pad))
    inner = jnp.where(dist <= radius, -slope * (dist * dilation).astype(F32), NEG_INF)
    if n_blocks == 1:
        return inner[None]
    first = jnp.where(col < pad, NEG_INF, inner)
    last = jnp.where(col >= length - (n_blocks - 1) * qblk + pad, NEG_INF, inner)
    return jnp.stack([first] + [inner] * (n_blocks - 2) + [last])


def _windows(x, n_blocks, qblk, radius):
    pad = _key_pad(n_blocks, radius)
    if pad == 0:
        return x[None]
    zeros = jnp.zeros((pad, x.shape[1]), x.dtype)
    xp = jnp.concatenate([zeros, x, zeros], axis=0)
    return jnp.stack([xp[t * qblk:t * qblk + qblk + 2 * pad] for t in range(n_blocks)])


def _block_softmax_pv(q, k, v, bias):
    scores = jnp.einsum("bqc,bkc->bqk", q, k, preferred_element_type=F32) * (HEAD_DIM ** -0.5)
    logits = scores + bias
    mx = jnp.max(logits, axis=-1, keepdims=True)
    p = jnp.exp(logits - mx)
    den = jnp.sum(p, axis=-1, keepdims=True)
    pv = jnp.einsum("bqk,bkc->bqc", p.astype(BF16), v, preferred_element_type=F32)
    return mx, den, pv


def _dil_attn_kernel(*refs, seq):
    qkv_refs = refs[:3 * C_GROUPS]
    qg_ref, kg_ref, sl_ref, o_ref, q_s, k_s, v_s, m_s, l_s, acc_s = refs[3 * C_GROUPS:]
    slope = sl_ref[:, 0:1]
    lanes = (HEAD_DIM,)
    for g, (window, dilation) in enumerate(C_PAIRS):
        radius = window // (2 * dilation)
        length = seq // dilation
        qblk = min(length, 128)
        nb = length // qblk
        q_ref, k_ref, v_ref = qkv_refs[3 * g:3 * g + 3]
        qn = _head_norm(q_ref[...], qg_ref[g:g + 1, :])
        kn = _head_norm(k_ref[...], kg_ref[g:g + 1, :])

        def residue(ref, r):
            return ref[pl.ds(r, length, stride=dilation), :]

        if dilation == 1:
            qr, kr, vr = [qn.astype(BF16)], [kn.astype(BF16)], [v_ref[...]]
        else:
            q_s[...] = qn
            k_s[...] = kn
            v_s[...] = v_ref[...].astype(F32)
            qr = [residue(q_s, r).astype(BF16) for r in range(dilation)]
            kr = [residue(k_s, r).astype(BF16) for r in range(dilation)]
            vr = [residue(v_s, r).astype(BF16) for r in range(dilation)]
        qb = jnp.concatenate([x.reshape(nb, qblk, HEAD_DIM) for x in qr])
        kb = jnp.concatenate([_windows(x, nb, qblk, radius) for x in kr])
        vb = jnp.concatenate([_windows(x, nb, qblk, radius) for x in vr])
        bias = _band_bias(nb, qblk, radius, dilation, slope, length)
        bias = jnp.concatenate([bias] * dilation)
        mx, den, pv = _block_softmax_pv(qb, kb, vb, bias)
        mx = jnp.broadcast_to(mx, mx.shape[:2] + lanes)
        den = jnp.broadcast_to(den, den.shape[:2] + lanes)
        if g == 0:
            m_s[...] = mx.reshape(seq, HEAD_DIM)
            l_s[...] = den.reshape(seq, HEAD_DIM)
            acc_s[...] = pv.reshape(seq, HEAD_DIM)
            continue
        for r in range(dilation):
            rows = slice(r * nb, (r + 1) * nb)
            m_old = residue(m_s, r)
            m_blk = mx[rows].reshape(length, HEAD_DIM)
            m_new = jnp.maximum(m_old, m_blk)
            a = jnp.exp(m_old - m_new)
            b = jnp.exp(m_blk - m_new)
            l_new = a * residue(l_s, r) + b * den[rows].reshape(length, HEAD_DIM)
            acc_new = a * residue(acc_s, r) + b * pv[rows].reshape(length, HEAD_DIM)
            m_s[pl.ds(r, length, stride=dilation), :] = m_new
            l_s[pl.ds(r, length, stride=dilation), :] = l_new
            acc_s[pl.ds(r, length, stride=dilation), :] = acc_new
    o_ref[...] = (acc_s[...] / l_s[...]).astype(o_ref.dtype)


def dilated_mixture(z, q_gain, k_gain, n_seq, seq):
    for window, dilation in C_PAIRS:
        assert seq % (dilation * (window // (2 * dilation))) == 0
    slopes = 2.0 ** (-ALIBI_MAX * jnp.arange(1, C_HEADS + 1, dtype=F32) / C_HEADS)
    slopes = jnp.repeat(slopes, HEAD_DIM).reshape(1, C_WIDTH)
    heads_per_part = C_WIDTH // HEAD_DIM

    def qkv_spec(g, part):
        col = (g * 3 + part) * heads_per_part
        return pl.BlockSpec((seq, HEAD_DIM), lambda b, h: (b, col + h))

    gain_spec = pl.BlockSpec((C_GROUPS, HEAD_DIM), lambda b, h: (0, 0))
    state = pltpu.VMEM((seq, HEAD_DIM), F32)
    return pl.pallas_call(
        functools.partial(_dil_attn_kernel, seq=seq),
        grid=(n_seq, C_HEADS),
        in_specs=[qkv_spec(g, part) for g in range(C_GROUPS) for part in range(3)]
                 + [gain_spec, gain_spec, pl.BlockSpec((1, HEAD_DIM), lambda b, h: (0, h))],
        out_specs=pl.BlockSpec((seq, HEAD_DIM), lambda b, h: (b, h)),
        out_shape=jax.ShapeDtypeStruct((n_seq * seq, C_WIDTH), BF16),
        scratch_shapes=[state] * 6,
        compiler_params=_params("parallel", "parallel"),
        name="dilated_attention",
    )(*([z] * (3 * C_GROUPS)), q_gain.astype(F32), k_gain.astype(F32), slopes)


def _mem_attn_kernel(q_ref, kv_ref, qg_ref, kg_ref, o_ref):
    def heads(ref, c0):
        return [ref[:, c0 + h * HEAD_DIM:c0 + (h + 1) * HEAD_DIM] for h in range(MEM_HEADS)]

    qn = jnp.stack([_head_norm(x, qg_ref[...]).astype(BF16) for x in heads(q_ref, 0)])
    kn = jnp.stack([_head_norm(x, kg_ref[...]).astype(BF16) for x in heads(kv_ref, 0)])
    v = jnp.stack([x.astype(BF16) for x in heads(kv_ref, MEM_WIDTH)])
    s = jnp.einsum("hqc,hkc->hqk", qn, kn, preferred_element_type=F32) * (HEAD_DIM ** -0.5)
    mx = jnp.max(s, axis=-1, keepdims=True)
    e = jnp.exp(s - mx)
    p = e / jnp.sum(e, axis=-1, keepdims=True)
    o = jnp.einsum("hqk,hkc->hqc", p.astype(BF16), v, preferred_element_type=F32)
    for h in range(MEM_HEADS):
        o_ref[:, h * HEAD_DIM:(h + 1) * HEAD_DIM] = o[h].astype(o_ref.dtype)


def memory_attention(q, kv, q_gain, k_gain, n_seq, seq, n_mem):
    bm = _pick(seq, (1024, 512, 256))
    nb = seq // bm
    gain_spec = pl.BlockSpec((1, HEAD_DIM), lambda b, i: (0, 0))
    return pl.pallas_call(
        _mem_attn_kernel,
        grid=(n_seq, nb),
        in_specs=[pl.BlockSpec((bm, MEM_WIDTH), lambda b, i: (b * nb + i, 0)),
                  pl.BlockSpec((n_mem, 2 * MEM_WIDTH), lambda b, i: (b, 0)),
                  gain_spec, gain_spec],
        out_specs=pl.BlockSpec((bm, MEM_WIDTH), lambda b, i: (b * nb + i, 0)),
        out_shape=jax.ShapeDtypeStruct((n_seq * seq, MEM_WIDTH), BF16),
        compiler_params=_params("parallel", "arbitrary"),
        name="memory_attention",
    )(q, kv, q_gain.reshape(1, HEAD_DIM).astype(F32), k_gain.reshape(1, HEAD_DIM).astype(F32))


def _trunk(x, mem, p, depth, ffn_w=None):
    n_seq, seq, d = x.shape
    n_mem = mem.shape[1]
    x = x.reshape(n_seq * seq, d)
    xb = x.astype(BF16)
    memb = mem.reshape(n_seq * n_mem, d).astype(BF16)
    convert = ffn_w is None
    ffn_w = [] if convert else ffn_w
    for i in range(depth):
        kind, j = i % N_MIXERS, i // N_MIXERS
        w_in, w_out = p["w_in"][kind], p["w_out"][kind]
        y_width = w_in.shape[2] - MEM_WIDTH
        z, q_mem = matmul([xb], [p["b_w_in_dft"] if kind == 1 else w_in], layer=j, out_dtypes=[BF16],
                          n_cols=y_width, norm=True, act="gelu" if kind == 0 else None,
                          side=(w_in, y_width, MEM_WIDTH), name="in_proj")
        if kind == 0:
            y = gated_chunk_mlp(z, p["a_w_s"][j], p["a_b_s"][j], p["a_v_gain"][j])
        elif kind == 1:
            y = fourier_positions(z, n_seq, seq)
        else:
            y = dilated_mixture(z, p["c_q_gain"][j], p["c_k_gain"][j], n_seq, seq)
        kv = matmul([memb], [p["mem_w_kv"]], layer=i, out_dtypes=[F32], norm=True, name="mem_kv")
        mo = memory_attention(q_mem, kv, p["mem_q_gain"][i], p["mem_k_gain"][i], n_seq, seq, n_mem)
        outs = matmul([y, mo], [p["b_w_out_y"] if kind == 1 else w_out, w_out], layer=j, out_dtypes=[F32, BF16],
                      row0s=[0, y.shape[1]], res=x, cast=(p["w_ff1"], i, p["ffn_gain"]) if convert else None,
                      name="out_proj")
        x, xb = outs[:2]
        w1 = outs[2] if convert else ffn_w[i][0]
        outs = matmul([xb], [w1[None]], out_dtypes=[BF16], norm=True, act="relu2",
                      cast=(p["w_ff2"], i, None) if convert else None, name="ffn_up")
        f, w2 = outs if convert else (outs, ffn_w[i][1])
        if convert:
            ffn_w.append((w1, w2))
        if i + 1 < depth:
            x, xb = matmul_kred_res(f, w2[None], x, layer=0, with_bf16=True, name="ffn_down")
        else:
            x = matmul_kred_res(f, w2[None], x, layer=0, with_bf16=False, name="ffn_down")
    return x.reshape(n_seq, seq, d), ffn_w


def _fold_gain(w, g):
    return (g.astype(F32)[:, :, None] * w).astype(BF16)


@jax.jit
def kernel(x_prompt, x_sample, mem_prompt, mem_sample, mixer_norm, mem_norm, ffn_norm, mem_w_kv,
           mem_q_gain, mem_k_gain, w_ff1, w_ff2, a_w_in, a_w_out, a_w_s, a_b_s, a_v_gain,
           b_w_in, b_w_out, c_w_in, c_w_out, c_q_gain, c_k_gain):
    depth = mixer_norm.shape[0]
    w_in = [_fold_gain(w, mixer_norm[k::N_MIXERS]) for k, w in enumerate((a_w_in, b_w_in, c_w_in))]
    b_width = b_w_in.shape[2] - MEM_WIDTH
    w_out = [w.astype(BF16) for w in (a_w_out, b_w_out, c_w_out)]
    p = {
        "mem_q_gain": mem_q_gain, "mem_k_gain": mem_k_gain,
        "a_w_s": a_w_s, "a_b_s": a_b_s, "a_v_gain": a_v_gain,
        "c_q_gain": c_q_gain, "c_k_gain": c_k_gain,
        "mem_w_kv": _fold_gain(mem_w_kv, mem_norm),
        "w_ff1": w_ff1, "w_ff2": w_ff2, "ffn_gain": ffn_norm.astype(F32)[:, :, None],
        "w_in": w_in,
        "b_w_in_dft": jnp.stack([fold_channel_dft(w, b_width) for w in w_in[1]]),
        "w_out": w_out,
        "b_w_out_y": jnp.stack([fourier_row_order(w, b_width) for w in w_out[1]]),
    }
    y_prompt, ffn_w = _trunk(x_prompt, mem_prompt, p, depth)
    y_sample, _ = _trunk(x_sample, mem_sample, p, depth, ffn_w)
    return y_prompt, y_sample
```

```python
import functools
import math

import jax
import jax.numpy as jnp
from jax import lax
from jax.experimental import pallas as pl
from jax.experimental.pallas import tpu as pltpu

F32 = jnp.float32
BF16 = jnp.bfloat16

EPS = 1e-6
NEG_INF = -1e30
HEAD_DIM = 128
MEM_HEADS = 4
MEM_WIDTH = MEM_HEADS * HEAD_DIM
A_CHUNK = 128
A_GROUPS = 8
B_GROUPS = 4
C_PAIRS = ((128, 1), (512, 4), (2048, 16))
C_GROUPS = len(C_PAIRS)
C_HEADS = 8
C_WIDTH = C_HEADS * HEAD_DIM
ALIBI_MAX = 8.0
N_MIXERS = 3

V7X_VMEM_BYTES = 64 * 1024 * 1024
VMEM_LIMIT_BYTES = V7X_VMEM_BYTES - 2 * 1024 * 1024
V7X_LANES = 128


def _params(*semantics):
    return pltpu.CompilerParams(dimension_semantics=semantics, vmem_limit_bytes=VMEM_LIMIT_BYTES)


def _pick(n, candidates):
    for c in candidates:
        if n % c == 0:
            return c
    raise ValueError(f"no block size in {candidates} divides {n}")


def _gelu(x):
    return 0.5 * x * (1.0 + lax.erf(x * (1.0 / math.sqrt(2.0))))


def _cast_rows_kernel(x_ref, xb_ref, ssq_ref):
    x = x_ref[...]
    xb_ref[...] = x.astype(xb_ref.dtype)
    ssq_ref[...] = _lane_group_sumsq(x)


def cast_rows(x):
    m, d = x.shape
    bm = _pick(m, (512, 256, 128, 8))
    return pl.pallas_call(
        _cast_rows_kernel,
        grid=(m // bm,),
        in_specs=[pl.BlockSpec((bm, d), lambda i: (i, 0))],
        out_specs=[pl.BlockSpec((bm, d), lambda i: (i, 0)), pl.BlockSpec((bm, V7X_LANES), lambda i: (i, 0))],
        out_shape=[jax.ShapeDtypeStruct((m, d), BF16), jax.ShapeDtypeStruct((m, V7X_LANES), F32)],
        compiler_params=_params("parallel"),
        name="cast_rows",
    )(x)


def _row_inv_rms(a_ref):
    rows, k = a_ref.shape
    part = jnp.zeros((rows, V7X_LANES), F32)
    for c in range(k // V7X_LANES):
        blk = a_ref[:, c * V7X_LANES:(c + 1) * V7X_LANES].astype(F32)
        part = part + blk * blk
    return lax.rsqrt(jnp.sum(part, axis=-1, keepdims=True) / k + EPS)


def _lane_group_sumsq(x):
    sq = x * x
    part = sq[:, :V7X_LANES]
    for c in range(1, x.shape[1] // V7X_LANES):
        part = part + sq[:, c * V7X_LANES:(c + 1) * V7X_LANES]
    return part


def _accumulate_over_j(ref, part, first):
    @pl.when(first)
    def _():
        ref[...] = part

    @pl.when(jnp.logical_not(first))
    def _():
        ref[...] += part


def _mm_kernel(*refs, n_ops, norm, act, has_res, n_out, side, casts, ssq_out):
    a_refs = refs[:n_ops]
    w_refs = refs[n_ops:2 * n_ops]
    pos = 2 * n_ops
    ws_ref = refs[pos] if side else None
    pos += side
    ssq_in_ref = refs[pos] if norm else None
    pos += norm
    res_ref = refs[pos] if has_res else None
    pos += has_res
    cast_in = []
    for has_gain in casts:
        cast_in.append(refs[pos:pos + 1 + has_gain])
        pos += 1 + has_gain
    o_refs = refs[pos:pos + n_out]
    pos += n_out
    os_ref = refs[pos] if side else None
    pos += side
    for src_ref, *gain_ref in cast_in:
        src = src_ref[...]
        refs[pos][...] = (src * gain_ref[0][...] if gain_ref else src).astype(BF16)
        pos += 1
    ssq_out_ref = refs[pos] if ssq_out else None
    pos += ssq_out
    first_col = pl.program_id(1) == 0
    if norm:
        inv_ref = refs[pos]

        @pl.when(first_col)
        def _():
            ms = jnp.sum(ssq_in_ref[...], axis=-1, keepdims=True) / a_refs[0].shape[1]
            inv = lax.rsqrt(ms + EPS)
            inv_ref[...] = jnp.broadcast_to(inv, inv_ref.shape)
            if side:
                d = jnp.dot(a_refs[0][...], ws_ref[...], preferred_element_type=F32)
                os_ref[...] = (d * inv).astype(os_ref.dtype)

    acc = None
    for a_ref, w_ref in zip(a_refs, w_refs):
        d = jnp.dot(a_ref[...], w_ref[...], preferred_element_type=F32)
        acc = d if acc is None else acc + d
    if norm:
        acc = acc * inv_ref[:, 0:1]
    if act == "relu2":
        r = jnp.maximum(acc, 0.0)
        acc = r * r
    elif act == "gelu":
        acc = _gelu(acc)
    if has_res:
        acc = res_ref[...] + acc
    for o_ref in o_refs:
        o_ref[...] = acc.astype(o_ref.dtype)
    if ssq_out:
        _accumulate_over_j(ssq_out_ref, _lane_group_sumsq(acc), first_col)


CAST_TILE_COLS = 1024


def _cast_tiles(steps, k, n):
    ntc = n // CAST_TILE_COLS
    ntr = 1
    while ntr * 2 * ntc <= steps and k % (ntr * 2) == 0:
        ntr *= 2
    assert ntr * ntc <= steps and n % CAST_TILE_COLS == 0
    return ntr, ntc


def _mm_blocks(m, n, k_total, out_bytes, has_res, side_cols, cast_shapes):
    for bm, bn in ((1024, 1024), (1024, 512), (512, 1024), (512, 512), (256, 512)):
        if m % bm or n % bn:
            continue
        windows = 2 * (bm * k_total * 2 + k_total * bn * 2 + bm * bn * (out_bytes + 4 * has_res)
                       + (k_total + bm) * side_cols * 2)
        for ck, cn in cast_shapes:
            ntr, _ = _cast_tiles((m // bm) * (n // bn), ck, cn)
            windows += 2 * (ck // ntr) * (CAST_TILE_COLS * 6 + V7X_LANES * 4)
        if windows + 3 * bm * bn * 4 <= VMEM_LIMIT_BYTES:
            return bm, bn
    raise ValueError(f"no matmul blocks fit m={m} n={n} k={k_total}")


def matmul(a_list, w_list, *, out_dtypes, layer=0, n_cols=None, col0=0, row0s=None, norm=False, act=None,
           res=None, side=None, casts=(), ssq=None, ssq_out=False, name="matmul"):
    m = a_list[0].shape[0]
    n = n_cols if n_cols is not None else w_list[0].shape[2]
    row0s = row0s or [0] * len(a_list)
    ks = [a.shape[1] for a in a_list]
    out_bytes = sum(jnp.dtype(d).itemsize for d in out_dtypes)
    side_cols = side[2] if side else 0
    bm, bn = _mm_blocks(m, n, sum(ks), out_bytes, res is not None, side_cols, [c[0].shape[1:] for c in casts])
    assert col0 % bn == 0 and all(r % k == 0 for r, k in zip(row0s, ks))
    assert not side or (norm and len(a_list) == 1 and side[1] % side[2] == 0)
    assert norm == (ssq is not None)
    cb = col0 // bn
    in_specs = [pl.BlockSpec((bm, k), lambda i, j: (i, 0)) for k in ks]
    in_specs += [pl.BlockSpec((None, k, bn), functools.partial(lambda i, j, rb: (layer, rb, cb + j), rb=r // k))
                 for r, k in zip(row0s, ks)]
    args = list(a_list) + list(w_list)
    out_specs = [pl.BlockSpec((bm, bn), lambda i, j: (i, j))] * len(out_dtypes)
    out_shape = [jax.ShapeDtypeStruct((m, n), d) for d in out_dtypes]
    if side:
        sb = side[1] // side_cols
        in_specs.append(pl.BlockSpec((None, ks[0], side_cols), lambda i, j: (layer, 0, sb)))
        args.append(side[0])
        out_specs.append(pl.BlockSpec((bm, side_cols), lambda i, j: (i, 0)))
        out_shape.append(jax.ShapeDtypeStruct((m, side_cols), BF16))
    ssq_spec = pl.BlockSpec((bm, V7X_LANES), lambda i, j: (i, 0))
    if norm:
        in_specs.append(ssq_spec)
        args.append(ssq)
    if res is not None:
        in_specs.append(pl.BlockSpec((bm, bn), lambda i, j: (i, j)))
        args.append(res)
    nj = n // bn
    for src, src_layer, gain in casts:
        ck, cn = src.shape[1:]
        ntr, ntc = _cast_tiles((m // bm) * nj, ck, cn)
        tr = ck // ntr

        def tile(i, j, ntr=ntr, ntc=ntc):
            t = jnp.minimum(i * nj + j, ntr * ntc - 1)
            return t // ntc, t % ntc

        in_specs.append(pl.BlockSpec((None, tr, CAST_TILE_COLS),
                                     functools.partial(lambda i, j, l, tile: (l, *tile(i, j)), l=src_layer, tile=tile)))
        args.append(src)
        if gain is not None:
            in_specs.append(pl.BlockSpec((None, tr, 1),
                                         functools.partial(lambda i, j, l, tile: (l, tile(i, j)[0], 0),
                                                           l=src_layer, tile=tile)))
            args.append(gain)
        out_specs = out_specs + [pl.BlockSpec((tr, CAST_TILE_COLS), tile)]
        out_shape = out_shape + [jax.ShapeDtypeStruct((ck, cn), BF16)]
    if ssq_out:
        out_specs = out_specs + [ssq_spec]
        out_shape = out_shape + [jax.ShapeDtypeStruct((m, V7X_LANES), F32)]
    outs = pl.pallas_call(
        functools.partial(_mm_kernel, n_ops=len(a_list), norm=norm, act=act, has_res=res is not None,
                          n_out=len(out_dtypes), side=bool(side),
                          casts=tuple(c[2] is not None for c in casts), ssq_out=ssq_out),
        grid=(m // bm, n // bn),
        in_specs=in_specs,
        out_specs=out_specs,
        out_shape=out_shape,
        scratch_shapes=[pltpu.VMEM((bm, V7X_LANES), F32)] if norm else [],
        compiler_params=_params("parallel", "arbitrary"),
        name=name,
    )(*args)
    return outs if len(outs) > 1 else outs[0]


def _mm_kred_kernel(a_ref, w_ref, res_ref, o_ref, *next_layer_refs):
    kk = pl.program_id(2)

    def dot():
        return jnp.dot(a_ref[...], w_ref[...], preferred_element_type=F32)

    @pl.when(kk == 0)
    def _():
        o_ref[...] = res_ref[...] + dot()

    @pl.when(kk > 0)
    def _():
        o_ref[...] += dot()

    if next_layer_refs:
        ob_ref, ssq_ref = next_layer_refs
        last = kk == pl.num_programs(2) - 1
        first_col = pl.program_id(1) == 0

        @pl.when(last)
        def _():
            ob_ref[...] = o_ref[...].astype(ob_ref.dtype)

        @pl.when(jnp.logical_and(last, first_col))
        def _():
            ssq_ref[...] = _lane_group_sumsq(o_ref[...])

        @pl.when(jnp.logical_and(last, jnp.logical_not(first_col)))
        def _():
            ssq_ref[...] += _lane_group_sumsq(o_ref[...])


def matmul_kred_res(a, w, res, *, layer, with_bf16, name="matmul_kred"):
    m, k = a.shape
    n = w.shape[2]
    bm = _pick(m, (1024, 512, 256))
    bn = _pick(n, (1024, 512))
    bk = _pick(k, (4096, 2048, 1024))
    out_spec = pl.BlockSpec((bm, bn), lambda i, j, kk: (i, j))
    out_shape = [jax.ShapeDtypeStruct((m, n), F32)]
    out_specs = [out_spec]
    if with_bf16:
        out_shape += [jax.ShapeDtypeStruct((m, n), BF16), jax.ShapeDtypeStruct((m, V7X_LANES), F32)]
        out_specs += [out_spec, pl.BlockSpec((bm, V7X_LANES), lambda i, j, kk: (i, 0))]
    outs = pl.pallas_call(
        _mm_kred_kernel,
        grid=(m // bm, n // bn, k // bk),
        in_specs=[pl.BlockSpec((bm, bk), lambda i, j, kk: (i, kk)),
                  pl.BlockSpec((None, bk, bn), lambda i, j, kk: (layer, kk, j)),
                  pl.BlockSpec((bm, bn), lambda i, j, kk: (i, j))],
        out_specs=out_specs,
        out_shape=out_shape,
        compiler_params=_params("parallel", "arbitrary", "arbitrary"),
        name=name,
    )(a, w, res)
    return outs if with_bf16 else outs[0]


def _gate_kernel(u_ref, v_ref, vg_ref, ws_ref, bs_ref, o_ref, *, n_chunks, gw):
    inv = _row_inv_rms(v_ref)

    def gate(g, carry):
        c0 = pl.multiple_of(g * gw, gw)
        vn = ((v_ref[:, pl.ds(c0, gw)].astype(F32) * inv) * vg_ref[:, pl.ds(c0, gw)]).astype(BF16)
        u = u_ref[:, pl.ds(c0, gw)].astype(F32)
        w = ws_ref[g]
        b = bs_ref[g]
        for c in range(n_chunks):
            r0 = c * A_CHUNK
            vm = jnp.dot(w, vn[r0:r0 + A_CHUNK], preferred_element_type=F32) + b
            o_ref[r0:r0 + A_CHUNK, pl.ds(c0, gw)] = (u[r0:r0 + A_CHUNK] * vm).astype(o_ref.dtype)
        return carry

    lax.fori_loop(0, A_GROUPS, gate, 0)


def gated_chunk_mlp(uv, w_s, b_s, v_gain):
    m, w2 = uv.shape
    width = w2 // 2
    gw = width // A_GROUPS
    rows = _pick(m, (512, 256, 128))
    return pl.pallas_call(
        functools.partial(_gate_kernel, n_chunks=rows // A_CHUNK, gw=gw),
        grid=(m // rows,),
        in_specs=[pl.BlockSpec((rows, width), lambda i: (i, 0)),
                  pl.BlockSpec((rows, width), lambda i: (i, 1)),
                  pl.BlockSpec((1, width), lambda i: (0, 0)),
                  pl.BlockSpec((A_GROUPS, A_CHUNK, A_CHUNK), lambda i: (0, 0, 0)),
                  pl.BlockSpec((A_GROUPS, A_CHUNK, 1), lambda i: (0, 0, 0))],
        out_specs=pl.BlockSpec((rows, width), lambda i: (i, 0)),
        out_shape=jax.ShapeDtypeStruct((m, width), BF16),
        compiler_params=_params("parallel"),
        name="gated_chunk_mlp",
    )(uv, uv, v_gain.reshape(1, width).astype(F32), w_s.astype(BF16),
      b_s.reshape(A_GROUPS, A_CHUNK, 1).astype(F32))


def _dft_tables(n):
    idx = lax.iota(jnp.int32, n)
    prod = (idx[:, None] * idx[None, :]) % n
    theta = prod.astype(F32) * (2.0 * math.pi / n)
    scale = 1.0 / math.sqrt(n)
    return jnp.cos(theta) * scale, -jnp.sin(theta) * scale


def _half_channel_dft(gc):
    cos, nsin = _dft_tables(gc)
    half = gc // 2
    return jnp.concatenate([cos[:, :half + 1], -nsin[:, 1:half]], axis=1)


def _fourier_channel_order(gc):
    half = gc // 2
    return list(range(half + 1)) + list(range(gc - 1, half, -1))


def fourier_row_order(w_out, width):
    n = w_out.shape[1]
    gc = width // B_GROUPS
    order = jnp.asarray(_fourier_channel_order(gc), jnp.int32)
    onehot = (order[:, None] == lax.iota(jnp.int32, gc)[None, :]).astype(BF16)
    bn = _pick(n, (1024, 512))
    return pl.pallas_call(
        _chan_dft_kernel,
        grid=(B_GROUPS, n // bn),
        in_specs=[pl.BlockSpec((gc, gc), lambda g, j: (0, 0)),
                  pl.BlockSpec((gc, bn), lambda g, j: (g, j))],
        out_specs=pl.BlockSpec((gc, bn), lambda g, j: (g, j)),
        out_shape=jax.ShapeDtypeStruct((width, n), BF16),
        compiler_params=_params("parallel", "parallel"),
        name="fourier_row_order",
    )(onehot, w_out)


def _chan_dft_kernel(z_ref, w_ref, o_ref):
    o_ref[...] = jnp.dot(z_ref[...], w_ref[...], preferred_element_type=F32).astype(o_ref.dtype)


def fold_channel_dft(w_in, width):
    k = w_in.shape[0]
    gc = width // B_GROUPS
    bm = _pick(k, (1024, 512, 256))
    return pl.pallas_call(
        _chan_dft_kernel,
        grid=(k // bm, B_GROUPS),
        in_specs=[pl.BlockSpec((bm, gc), lambda i, g: (i, g)),
                  pl.BlockSpec((gc, gc), lambda i, g: (0, 0))],
        out_specs=pl.BlockSpec((bm, gc), lambda i, g: (i, g)),
        out_shape=jax.ShapeDtypeStruct((k, width), BF16),
        compiler_params=_params("parallel", "parallel"),
        name="fold_channel_dft",
    )(w_in, _half_channel_dft(gc).astype(BF16))


def _pos_dft_kernel(c_ref, ns_ref, pc_ref, pq_ref, o_ref):
    half = pc_ref.shape[1]
    a = jnp.dot(c_ref[...], pc_ref[...], preferred_element_type=F32)
    b = jnp.dot(ns_ref[...], pq_ref[...], preferred_element_type=F32)
    nyq = jnp.dot(c_ref[...], pq_ref[:, :V7X_LANES], preferred_element_type=F32)[:, 0:1]
    first = lax.broadcasted_iota(jnp.int32, a.shape, 1) == 0
    o_ref[:, :half] = (a + jnp.where(first, 0.0, b)).astype(o_ref.dtype)
    o_ref[:, half:] = jnp.where(first, nyq, a - b).astype(o_ref.dtype)


def fourier_positions(p, n_seq, seq):
    m, width = p.shape
    gc = width // B_GROUPS
    half = gc // 2
    cos, nsin = _dft_tables(seq)
    bo = _pick(seq, (1024, 512, 256))
    nb = seq // bo
    return pl.pallas_call(
        _pos_dft_kernel,
        grid=(n_seq, B_GROUPS, nb),
        in_specs=[pl.BlockSpec((bo, seq), lambda b, g, i: (i, 0)),
                  pl.BlockSpec((bo, seq), lambda b, g, i: (i, 0)),
                  pl.BlockSpec((seq, half), lambda b, g, i: (b, 2 * g)),
                  pl.BlockSpec((seq, half), lambda b, g, i: (b, 2 * g + 1))],
        out_specs=pl.BlockSpec((bo, gc), lambda b, g, i: (b * nb + i, g)),
        out_shape=jax.ShapeDtypeStruct((m, width), BF16),
        compiler_params=_params("parallel", "parallel", "arbitrary"),
        name="fourier_positions",
    )(cos.astype(BF16), nsin.astype(BF16), p, p)


def _head_norm(x, g):
    x = x.astype(F32)
    ms = jnp.mean(x * x, axis=-1, keepdims=True)
    return (x * lax.rsqrt(ms + EPS)) * g


def _key_pad(n_blocks, radius):
    return 0 if n_blocks == 1 else radius


def _band_bias(n_blocks, qblk, radius, dilation, slope, length):
    pad = _key_pad(n_blocks, radius)
    kw = qblk + 2 * pad
    row = lax.broadcasted_iota(jnp.int32, (qblk, kw), 0)
    col = lax.broadcasted_iota(jnp.int32, (qblk, kw), 1)
    dist = jnp.abs(row - (col - pad))
    inner = jnp.where(dist <= radius, -slope * (dist * dilation).astype(F32), NEG_INF)
    if n_blocks == 1:
        return inner[None]
    first = jnp.where(col < pad, NEG_INF, inner)
    last = jnp.where(col >= length - (n_blocks - 1) * qblk + pad, NEG_INF, inner)
    return jnp.stack([first] + [inner] * (n_blocks - 2) + [last])


def _windows(x, n_blocks, qblk, radius):
    pad = _key_pad(n_blocks, radius)
    if pad == 0:
        return x[None]
    zeros = jnp.zeros((pad, x.shape[1]), x.dtype)
    xp = jnp.concatenate([zeros, x, zeros], axis=0)
    return jnp.stack([xp[t * qblk:t * qblk + qblk + 2 * pad] for t in range(n_blocks)])


def _block_softmax_pv(q, k, v, bias):
    scores = jnp.einsum("bqc,bkc->bqk", q, k, preferred_element_type=F32) * (HEAD_DIM ** -0.5)
    logits = scores + bias
    mx = jnp.max(logits, axis=-1, keepdims=True)
    p = jnp.exp(logits - mx)
    den = jnp.sum(p, axis=-1, keepdims=True)
    pv = jnp.einsum("bqk,bkc->bqc", p.astype(BF16), v, preferred_element_type=F32)
    return mx, den, pv


def _dil_attn_kernel(*refs, seq):
    qkv_refs = refs[:3 * C_GROUPS]
    qg_ref, kg_ref, sl_ref, o_ref, q_s, k_s, v_s, m_s, l_s, acc_s = refs[3 * C_GROUPS:]
    slope = sl_ref[:, 0:1]
    lanes = (HEAD_DIM,)
    for g, (window, dilation) in enumerate(C_PAIRS):
        radius = window // (2 * dilation)
        length = seq // dilation
        qblk = min(length, 128)
        nb = length // qblk
        q_ref, k_ref, v_ref = qkv_refs[3 * g:3 * g + 3]
        qn = _head_norm(q_ref[...], qg_ref[g:g + 1, :])
        kn = _head_norm(k_ref[...], kg_ref[g:g + 1, :])

        def residue(ref, r):
            return ref[pl.ds(r, length, stride=dilation), :]

        if dilation == 1:
            qr, kr, vr = [qn.astype(BF16)], [kn.astype(BF16)], [v_ref[...]]
        else:
            q_s[...] = qn
            k_s[...] = kn
            v_s[...] = v_ref[...].astype(F32)
            qr = [residue(q_s, r).astype(BF16) for r in range(dilation)]
            kr = [residue(k_s, r).astype(BF16) for r in range(dilation)]
            vr = [residue(v_s, r).astype(BF16) for r in range(dilation)]
        qb = jnp.concatenate([x.reshape(nb, qblk, HEAD_DIM) for x in qr])
        kb = jnp.concatenate([_windows(x, nb, qblk, radius) for x in kr])
        vb = jnp.concatenate([_windows(x, nb, qblk, radius) for x in vr])
        bias = _band_bias(nb, qblk, radius, dilation, slope, length)
        bias = jnp.concatenate([bias] * dilation)
        mx, den, pv = _block_softmax_pv(qb, kb, vb, bias)
        mx = jnp.broadcast_to(mx, mx.shape[:2] + lanes)
        den = jnp.broadcast_to(den, den.shape[:2] + lanes)
        if g == 0:
            m_s[...] = mx.reshape(seq, HEAD_DIM)
            l_s[...] = den.reshape(seq, HEAD_DIM)
            acc_s[...] = pv.reshape(seq, HEAD_DIM)
            continue
        for r in range(dilation):
            rows = slice(r * nb, (r + 1) * nb)
            m_old = residue(m_s, r)
            m_blk = mx[rows].reshape(length, HEAD_DIM)
            m_new = jnp.maximum(m_old, m_blk)
            a = jnp.exp(m_old - m_new)
            b = jnp.exp(m_blk - m_new)
            l_new = a * residue(l_s, r) + b * den[rows].reshape(length, HEAD_DIM)
            acc_new = a * residue(acc_s, r) + b * pv[rows].reshape(length, HEAD_DIM)
            m_s[pl.ds(r, length, stride=dilation), :] = m_new
            l_s[pl.ds(r, length, stride=dilation), :] = l_new
            acc_s[pl.ds(r, length, stride=dilation), :] = acc_new
    o_ref[...] = (acc_s[...] / l_s[...]).astype(o_ref.dtype)


def dilated_mixture(z, q_gain, k_gain, n_seq, seq):
    for window, dilation in C_PAIRS:
        assert seq % (dilation * (window // (2 * dilation))) == 0
    slopes = 2.0 ** (-ALIBI_MAX * jnp.arange(1, C_HEADS + 1, dtype=F32) / C_HEADS)
    slopes = jnp.repeat(slopes, HEAD_DIM).reshape(1, C_WIDTH)
    heads_per_part = C_WIDTH // HEAD_DIM

    def qkv_spec(g, part):
        col = (g * 3 + part) * heads_per_part
        return pl.BlockSpec((seq, HEAD_DIM), lambda b, h: (b, col + h))

    gain_spec = pl.BlockSpec((C_GROUPS, HEAD_DIM), lambda b, h: (0, 0))
    state = pltpu.VMEM((seq, HEAD_DIM), F32)
    return pl.pallas_call(
        functools.partial(_dil_attn_kernel, seq=seq),
        grid=(n_seq, C_HEADS),
        in_specs=[qkv_spec(g, part) for g in range(C_GROUPS) for part in range(3)]
                 + [gain_spec, gain_spec, pl.BlockSpec((1, HEAD_DIM), lambda b, h: (0, h))],
        out_specs=pl.BlockSpec((seq, HEAD_DIM), lambda b, h: (b, h)),
        out_shape=jax.ShapeDtypeStruct((n_seq * seq, C_WIDTH), BF16),
        scratch_shapes=[state] * 6,
        compiler_params=_params("parallel", "parallel"),
        name="dilated_attention",
    )(*([z] * (3 * C_GROUPS)), q_gain.astype(F32), k_gain.astype(F32), slopes)


def _mem_attn_kernel(q_ref, kv_ref, qg_ref, kg_ref, o_ref):
    def heads(ref, c0):
        return [ref[:, c0 + h * HEAD_DIM:c0 + (h + 1) * HEAD_DIM] for h in range(MEM_HEADS)]

    qn = jnp.stack([_head_norm(x, qg_ref[...]).astype(BF16) for x in heads(q_ref, 0)])
    kn = jnp.stack([_head_norm(x, kg_ref[...]).astype(BF16) for x in heads(kv_ref, 0)])
    v = jnp.stack([x.astype(BF16) for x in heads(kv_ref, MEM_WIDTH)])
    s = jnp.einsum("hqc,hkc->hqk", qn, kn, preferred_element_type=F32) * (HEAD_DIM ** -0.5)
    mx = jnp.max(s, axis=-1, keepdims=True)
    e = jnp.exp(s - mx)
    p = e / jnp.sum(e, axis=-1, keepdims=True)
    o = jnp.einsum("hqk,hkc->hqc", p.astype(BF16), v, preferred_element_type=F32)
    for h in range(MEM_HEADS):
        o_ref[:, h * HEAD_DIM:(h + 1) * HEAD_DIM] = o[h].astype(o_ref.dtype)


def memory_attention(q, kv, q_gain, k_gain, n_seq, seq, n_mem):
    bm = _pick(seq, (1024, 512, 256))
    nb = seq // bm
    gain_spec = pl.BlockSpec((1, HEAD_DIM), lambda b, i: (0, 0))
    return pl.pallas_call(
        _mem_attn_kernel,
        grid=(n_seq, nb),
        in_specs=[pl.BlockSpec((bm, MEM_WIDTH), lambda b, i: (b * nb + i, 0)),
                  pl.BlockSpec((n_mem, 2 * MEM_WIDTH), lambda b, i: (b, 0)),
                  gain_spec, gain_spec],
        out_specs=pl.BlockSpec((bm, MEM_WIDTH), lambda b, i: (b * nb + i, 0)),
        out_shape=jax.ShapeDtypeStruct((n_seq * seq, MEM_WIDTH), BF16),
        compiler_params=_params("parallel", "arbitrary"),
        name="memory_attention",
    )(q, kv, q_gain.reshape(1, HEAD_DIM).astype(F32), k_gain.reshape(1, HEAD_DIM).astype(F32))


def _trunk(x, mem, p, depth, ffn_w=None):
    n_seq, seq, d = x.shape
    n_mem = mem.shape[1]
    x = x.reshape(n_seq * seq, d)
    xb, ssq = cast_rows(x)
    memb, mem_ssq = cast_rows(mem.reshape(n_seq * n_mem, d))
    convert = ffn_w is None
    ffn_w = [[None, None] for _ in range(depth)] if convert else ffn_w
    for i in range(depth):
        kind, j = i % N_MIXERS, i // N_MIXERS
        w_in, w_out = p["w_in"][kind], p["w_out"][kind]
        y_width = w_in.shape[2] - MEM_WIDTH
        z, q_mem = matmul([xb], [p["b_w_in_dft"] if kind == 1 else w_in], layer=j, out_dtypes=[BF16],
                          n_cols=y_width, norm=True, ssq=ssq, act="gelu" if kind == 0 else None,
                          side=(w_in, y_width, MEM_WIDTH), name="in_proj")
        if kind == 0:
            y = gated_chunk_mlp(z, p["a_w_s"][j], p["a_b_s"][j], p["a_v_gain"][j])
        elif kind == 1:
            y = fourier_positions(z, n_seq, seq)
        else:
            y = dilated_mixture(z, p["c_q_gain"][j], p["c_k_gain"][j], n_seq, seq)
        kv = matmul([memb], [p["mem_w_kv"]], layer=i, out_dtypes=[F32], norm=True, ssq=mem_ssq, name="mem_kv")
        mo = memory_attention(q_mem, kv, p["mem_q_gain"][i], p["mem_k_gain"][i], n_seq, seq, n_mem)
        first_ff1 = convert and i == 0
        outs = matmul([y, mo], [p["b_w_out_y"] if kind == 1 else w_out, w_out], layer=j, out_dtypes=[F32, BF16],
                      row0s=[0, y.shape[1]], res=x, ssq_out=True, name="out_proj",
                      casts=[(p["w_ff1"], 0, p["ffn_gain"])] if first_ff1 else ())
        x, xb, ssq = outs[0], outs[1], outs[-1]
        if first_ff1:
            ffn_w[0][0] = outs[2]
        riders = []
        if convert:
            riders = [(p["w_ff2"], i, None)] + ([(p["w_ff1"], i + 1, p["ffn_gain"])] if i + 1 < depth else [])
        outs = matmul([xb], [ffn_w[i][0][None]], out_dtypes=[BF16], norm=True, ssq=ssq, act="relu2", casts=riders,
                      name="ffn_up")
        if convert:
            f, ffn_w[i][1] = outs[0], outs[1]
            if i + 1 < depth:
                ffn_w[i + 1][0] = outs[2]
        else:
            f = outs
        w2 = ffn_w[i][1][None]
        if i + 1 < depth:
            x, xb, ssq = matmul_kred_res(f, w2, x, layer=0, with_bf16=True, name="ffn_down")
        else:
            x = matmul_kred_res(f, w2, x, layer=0, with_bf16=False, name="ffn_down")
    return x.reshape(n_seq, seq, d), ffn_w


def _fold_gain(w, g):
    return (g.astype(F32)[:, :, None] * w).astype(BF16)


@jax.jit
def kernel(x_prompt, x_sample, mem_prompt, mem_sample, mixer_norm, mem_norm, ffn_norm, mem_w_kv,
           mem_q_gain, mem_k_gain, w_ff1, w_ff2, a_w_in, a_w_out, a_w_s, a_b_s, a_v_gain,
           b_w_in, b_w_out, c_w_in, c_w_out, c_q_gain, c_k_gain):
    depth = mixer_norm.shape[0]
    w_in = [_fold_gain(w, mixer_norm[k::N_MIXERS]) for k, w in enumerate((a_w_in, b_w_in, c_w_in))]
    b_width = b_w_in.shape[2] - MEM_WIDTH
    w_out = [w.astype(BF16) for w in (a_w_out, b_w_out, c_w_out)]
    p = {
        "mem_q_gain": mem_q_gain, "mem_k_gain": mem_k_gain,
        "a_w_s": a_w_s, "a_b_s": a_b_s, "a_v_gain": a_v_gain,
        "c_q_gain": c_q_gain, "c_k_gain": c_k_gain,
        "mem_w_kv": _fold_gain(mem_w_kv, mem_norm),
        "w_ff1": w_ff1, "w_ff2": w_ff2, "ffn_gain": ffn_norm.astype(F32)[:, :, None],
        "w_in": w_in,
        "b_w_in_dft": jnp.stack([fold_channel_dft(w, b_width) for w in w_in[1]]),
        "w_out": w_out,
        "b_w_out_y": jnp.stack([fourier_row_order(w, b_width) for w in w_out[1]]),
    }
    y_prompt, ffn_w = _trunk(x_prompt, mem_prompt, p, depth)
    y_sample, _ = _trunk(x_sample, mem_sample, p, depth, ffn_w)
    return y_prompt, y_sample
```

```python
import functools
import math

import jax
import jax.numpy as jnp
from jax import lax
from jax.experimental import pallas as pl
from jax.experimental.pallas import tpu as pltpu

F32 = jnp.float32
BF16 = jnp.bfloat16

EPS = 1e-6
NEG_INF = -1e30
HEAD_DIM = 128
MEM_HEADS = 4
MEM_WIDTH = MEM_HEADS * HEAD_DIM
A_CHUNK = 128
A_GROUPS = 8
B_GROUPS = 4
C_PAIRS = ((128, 1), (512, 4), (2048, 16))
C_GROUPS = len(C_PAIRS)
C_HEADS = 8
C_WIDTH = C_HEADS * HEAD_DIM
ALIBI_MAX = 8.0
N_MIXERS = 3

V7X_VMEM_BYTES = 64 * 1024 * 1024
VMEM_LIMIT_BYTES = V7X_VMEM_BYTES - 2 * 1024 * 1024
V7X_LANES = 128


def _params(*semantics):
    return pltpu.CompilerParams(dimension_semantics=semantics, vmem_limit_bytes=VMEM_LIMIT_BYTES)


def _pick(n, candidates):
    for c in candidates:
        if n % c == 0:
            return c
    raise ValueError(f"no block size in {candidates} divides {n}")


def _gelu(x):
    return 0.5 * x * (1.0 + lax.erf(x * (1.0 / math.sqrt(2.0))))


def _stacked_specs(arrays, bm, width, axis_count):
    starts, specs, start = [], [], 0
    for a in arrays:
        nblk = a.shape[0] // bm
        assert a.shape[0] % bm == 0

        def index(i, *rest, start=start, nblk=nblk):
            return jnp.clip(i - start, 0, nblk - 1), rest[0] if axis_count > 1 else 0

        specs.append(pl.BlockSpec((bm, width), index))
        starts.append(start)
        start += nblk
    return specs, starts


def _select_stacked(refs, starts):
    i = pl.program_id(0)
    x = refs[0][...]
    for ref, start in zip(refs[1:], starts[1:]):
        x = jnp.where(i >= start, ref[...], x)
    return x


def _cast_rows_kernel(*refs, starts):
    xb_ref, ssq_ref = refs[-2:]
    x = _select_stacked(refs[:-2], starts)
    xb_ref[...] = x.astype(xb_ref.dtype)
    ssq_ref[...] = _lane_group_sumsq(x)


def cast_rows(xs):
    d = xs[0].shape[1]
    m = sum(x.shape[0] for x in xs)
    bm = _pick(math.gcd(*[x.shape[0] for x in xs]), (512, 256, 128, 8))
    in_specs, starts = _stacked_specs(xs, bm, d, 1)
    return pl.pallas_call(
        functools.partial(_cast_rows_kernel, starts=starts),
        grid=(m // bm,),
        in_specs=in_specs,
        out_specs=[pl.BlockSpec((bm, d), lambda i: (i, 0)), pl.BlockSpec((bm, V7X_LANES), lambda i: (i, 0))],
        out_shape=[jax.ShapeDtypeStruct((m, d), BF16), jax.ShapeDtypeStruct((m, V7X_LANES), F32)],
        compiler_params=_params("arbitrary"),
        name="cast_rows",
    )(*xs)


def _row_inv_rms(a_ref):
    rows, k = a_ref.shape
    part = jnp.zeros((rows, V7X_LANES), F32)
    for c in range(k // V7X_LANES):
        blk = a_ref[:, c * V7X_LANES:(c + 1) * V7X_LANES].astype(F32)
        part = part + blk * blk
    return lax.rsqrt(jnp.sum(part, axis=-1, keepdims=True) / k + EPS)


def _lane_group_sumsq(x):
    sq = x * x
    part = sq[:, :V7X_LANES]
    for c in range(1, x.shape[1] // V7X_LANES):
        part = part + sq[:, c * V7X_LANES:(c + 1) * V7X_LANES]
    return part


def _accumulate_over_j(ref, part, first):
    @pl.when(first)
    def _():
        ref[...] = part

    @pl.when(jnp.logical_not(first))
    def _():
        ref[...] += part


def _mm_kernel(*refs, n_ops, norm, act, res_starts, n_out, side, casts, ssq_out):
    a_refs = refs[:n_ops]
    w_refs = refs[n_ops:2 * n_ops]
    pos = 2 * n_ops
    ws_ref = refs[pos] if side else None
    pos += side
    ssq_in_ref = refs[pos] if norm else None
    pos += norm
    res_refs = refs[pos:pos + len(res_starts)]
    pos += len(res_starts)
    cast_in = []
    for has_gain in casts:
        cast_in.append(refs[pos:pos + 1 + has_gain])
        pos += 1 + has_gain
    o_refs = refs[pos:pos + n_out]
    pos += n_out
    os_ref = refs[pos] if side else None
    pos += side
    for src_ref, *gain_ref in cast_in:
        src = src_ref[...]
        refs[pos][...] = (src * gain_ref[0][...] if gain_ref else src).astype(BF16)
        pos += 1
    ssq_out_ref = refs[pos] if ssq_out else None
    pos += ssq_out
    first_col = pl.program_id(1) == 0
    if norm:
        inv_ref = refs[pos]

        @pl.when(first_col)
        def _():
            ms = jnp.sum(ssq_in_ref[...], axis=-1, keepdims=True) / a_refs[0].shape[1]
            inv = lax.rsqrt(ms + EPS)
            inv_ref[...] = jnp.broadcast_to(inv, inv_ref.shape)
            if side:
                d = jnp.dot(a_refs[0][...], ws_ref[...], preferred_element_type=F32)
                os_ref[...] = (d * inv).astype(os_ref.dtype)

    acc = None
    for a_ref, w_ref in zip(a_refs, w_refs):
        d = jnp.dot(a_ref[...], w_ref[...], preferred_element_type=F32)
        acc = d if acc is None else acc + d
    if norm:
        acc = acc * inv_ref[:, 0:1]
    if act == "relu2":
        r = jnp.maximum(acc, 0.0)
        acc = r * r
    elif act == "gelu":
        acc = _gelu(acc)
    if res_refs:
        acc = _select_stacked(res_refs, res_starts) + acc
    for o_ref in o_refs:
        o_ref[...] = acc.astype(o_ref.dtype)
    if ssq_out:
        _accumulate_over_j(ssq_out_ref, _lane_group_sumsq(acc), first_col)


CAST_TILE_COLS = 1024


def _cast_tiles(steps, k, n):
    ntc = n // CAST_TILE_COLS
    ntr = 1
    while ntr * 2 * ntc <= steps and k % (ntr * 2) == 0:
        ntr *= 2
    assert ntr * ntc <= steps and n % CAST_TILE_COLS == 0
    return ntr, ntc


def _mm_blocks(m, n, k_total, out_bytes, n_res, side_cols, cast_shapes):
    for bm, bn in ((1024, 1024), (1024, 512), (512, 1024), (512, 512), (256, 512)):
        if m % bm or n % bn:
            continue
        windows = 2 * (bm * k_total * 2 + k_total * bn * 2 + bm * bn * (out_bytes + 4 * n_res)
                       + (k_total + bm) * side_cols * 2)
        for ck, cn in cast_shapes:
            ntr, _ = _cast_tiles((m // bm) * (n // bn), ck, cn)
            windows += 2 * (ck // ntr) * (CAST_TILE_COLS * 6 + V7X_LANES * 4)
        if windows + 3 * bm * bn * 4 <= VMEM_LIMIT_BYTES:
            return bm, bn
    raise ValueError(f"no matmul blocks fit m={m} n={n} k={k_total}")


def matmul(a_list, w_list, *, out_dtypes, layer=0, n_cols=None, col0=0, row0s=None, norm=False, act=None,
           res=(), side=None, casts=(), ssq=None, ssq_out=False, name="matmul"):
    m = a_list[0].shape[0]
    n = n_cols if n_cols is not None else w_list[0].shape[2]
    row0s = row0s or [0] * len(a_list)
    ks = [a.shape[1] for a in a_list]
    out_bytes = sum(jnp.dtype(d).itemsize for d in out_dtypes)
    side_cols = side[2] if side else 0
    bm, bn = _mm_blocks(m, n, sum(ks), out_bytes, len(res), side_cols, [c[0].shape[1:] for c in casts])
    assert col0 % bn == 0 and all(r % k == 0 for r, k in zip(row0s, ks))
    assert not side or (norm and len(a_list) == 1 and side[1] % side[2] == 0)
    assert norm == (ssq is not None)
    cb = col0 // bn
    in_specs = [pl.BlockSpec((bm, k), lambda i, j: (i, 0)) for k in ks]
    in_specs += [pl.BlockSpec((None, k, bn), functools.partial(lambda i, j, rb: (layer, rb, cb + j), rb=r // k))
                 for r, k in zip(row0s, ks)]
    args = list(a_list) + list(w_list)
    out_specs = [pl.BlockSpec((bm, bn), lambda i, j: (i, j))] * len(out_dtypes)
    out_shape = [jax.ShapeDtypeStruct((m, n), d) for d in out_dtypes]
    if side:
        sb = side[1] // side_cols
        in_specs.append(pl.BlockSpec((None, ks[0], side_cols), lambda i, j: (layer, 0, sb)))
        args.append(side[0])
        out_specs.append(pl.BlockSpec((bm, side_cols), lambda i, j: (i, 0)))
        out_shape.append(jax.ShapeDtypeStruct((m, side_cols), BF16))
    ssq_spec = pl.BlockSpec((bm, V7X_LANES), lambda i, j: (i, 0))
    if norm:
        in_specs.append(ssq_spec)
        args.append(ssq)
    res_specs, res_starts = _stacked_specs(res, bm, bn, 2)
    in_specs += res_specs
    args += list(res)
    nj = n // bn
    for src, src_layer, gain in casts:
        ck, cn = src.shape[1:]
        ntr, ntc = _cast_tiles((m // bm) * nj, ck, cn)
        tr = ck // ntr

        def tile(i, j, ntr=ntr, ntc=ntc):
            t = jnp.minimum(i * nj + j, ntr * ntc - 1)
            return t // ntc, t % ntc

        in_specs.append(pl.BlockSpec((None, tr, CAST_TILE_COLS),
                                     functools.partial(lambda i, j, l, tile: (l, *tile(i, j)), l=src_layer, tile=tile)))
        args.append(src)
        if gain is not None:
            in_specs.append(pl.BlockSpec((None, tr, 1),
                                         functools.partial(lambda i, j, l, tile: (l, tile(i, j)[0], 0),
                                                           l=src_layer, tile=tile)))
            args.append(gain)
        out_specs = out_specs + [pl.BlockSpec((tr, CAST_TILE_COLS), tile)]
        out_shape = out_shape + [jax.ShapeDtypeStruct((ck, cn), BF16)]
    if ssq_out:
        out_specs = out_specs + [ssq_spec]
        out_shape = out_shape + [jax.ShapeDtypeStruct((m, V7X_LANES), F32)]
    outs = pl.pallas_call(
        functools.partial(_mm_kernel, n_ops=len(a_list), norm=norm, act=act, res_starts=tuple(res_starts),
                          n_out=len(out_dtypes), side=bool(side),
                          casts=tuple(c[2] is not None for c in casts), ssq_out=ssq_out),
        grid=(m // bm, n // bn),
        in_specs=in_specs,
        out_specs=out_specs,
        out_shape=out_shape,
        scratch_shapes=[pltpu.VMEM((bm, V7X_LANES), F32)] if norm else [],
        compiler_params=_params("parallel", "arbitrary"),
        name=name,
    )(*args)
    return outs if len(outs) > 1 else outs[0]


def _mm_kred_kernel(a_ref, w_ref, res_ref, o_ref, *next_layer_refs):
    kk = pl.program_id(2)

    def dot():
        return jnp.dot(a_ref[...], w_ref[...], preferred_element_type=F32)

    @pl.when(kk == 0)
    def _():
        o_ref[...] = res_ref[...] + dot()

    @pl.when(kk > 0)
    def _():
        o_ref[...] += dot()

    if next_layer_refs:
        ob_ref, ssq_ref = next_layer_refs
        last = kk == pl.num_programs(2) - 1
        first_col = pl.program_id(1) == 0

        @pl.when(last)
        def _():
            ob_ref[...] = o_ref[...].astype(ob_ref.dtype)

        @pl.when(jnp.logical_and(last, first_col))
        def _():
            ssq_ref[...] = _lane_group_sumsq(o_ref[...])

        @pl.when(jnp.logical_and(last, jnp.logical_not(first_col)))
        def _():
            ssq_ref[...] += _lane_group_sumsq(o_ref[...])


def matmul_kred_res(a, w, res, *, layer, with_bf16, row0=0, rows=None, name="matmul_kred"):
    k = a.shape[1]
    m = rows if rows is not None else a.shape[0]
    n = w.shape[2]
    bm = _pick(math.gcd(m, row0) if row0 else m, (1024, 512, 256))
    bn = _pick(n, (1024, 512))
    bk = _pick(k, (4096, 2048, 1024))
    rb0 = row0 // bm
    out_spec = pl.BlockSpec((bm, bn), lambda i, j, kk: (i, j))
    out_shape = [jax.ShapeDtypeStruct((m, n), F32)]
    out_specs = [out_spec]
    if with_bf16:
        out_shape += [jax.ShapeDtypeStruct((m, n), BF16), jax.ShapeDtypeStruct((m, V7X_LANES), F32)]
        out_specs += [out_spec, pl.BlockSpec((bm, V7X_LANES), lambda i, j, kk: (i, 0))]
    outs = pl.pallas_call(
        _mm_kred_kernel,
        grid=(m // bm, n // bn, k // bk),
        in_specs=[pl.BlockSpec((bm, bk), lambda i, j, kk: (rb0 + i, kk)),
                  pl.BlockSpec((None, bk, bn), lambda i, j, kk: (layer, kk, j)),
                  pl.BlockSpec((bm, bn), lambda i, j, kk: (rb0 + i, j))],
        out_specs=out_specs,
        out_shape=out_shape,
        compiler_params=_params("parallel", "arbitrary", "arbitrary"),
        name=name,
    )(a, w, res)
    return outs if with_bf16 else outs[0]


def _gate_kernel(u_ref, v_ref, vg_ref, ws_ref, bs_ref, o_ref, *, n_chunks, gw):
    inv = _row_inv_rms(v_ref)

    def gate(g, carry):
        c0 = pl.multiple_of(g * gw, gw)
        vn = ((v_ref[:, pl.ds(c0, gw)].astype(F32) * inv) * vg_ref[:, pl.ds(c0, gw)]).astype(BF16)
        u = u_ref[:, pl.ds(c0, gw)].astype(F32)
        w = ws_ref[g]
        b = bs_ref[g]
        for c in range(n_chunks):
            r0 = c * A_CHUNK
            vm = jnp.dot(w, vn[r0:r0 + A_CHUNK], preferred_element_type=F32) + b
            o_ref[r0:r0 + A_CHUNK, pl.ds(c0, gw)] = (u[r0:r0 + A_CHUNK] * vm).astype(o_ref.dtype)
        return carry

    lax.fori_loop(0, A_GROUPS, gate, 0)


def gated_chunk_mlp(uv, w_s, b_s, v_gain):
    m, w2 = uv.shape
    width = w2 // 2
    gw = width // A_GROUPS
    rows = _pick(m, (512, 256, 128))
    return pl.pallas_call(
        functools.partial(_gate_kernel, n_chunks=rows // A_CHUNK, gw=gw),
        grid=(m // rows,),
        in_specs=[pl.BlockSpec((rows, width), lambda i: (i, 0)),
                  pl.BlockSpec((rows, width), lambda i: (i, 1)),
                  pl.BlockSpec((1, width), lambda i: (0, 0)),
                  pl.BlockSpec((A_GROUPS, A_CHUNK, A_CHUNK), lambda i: (0, 0, 0)),
                  pl.BlockSpec((A_GROUPS, A_CHUNK, 1), lambda i: (0, 0, 0))],
        out_specs=pl.BlockSpec((rows, width), lambda i: (i, 0)),
        out_shape=jax.ShapeDtypeStruct((m, width), BF16),
        compiler_params=_params("parallel"),
        name="gated_chunk_mlp",
    )(uv, uv, v_gain.reshape(1, width).astype(F32), w_s.astype(BF16),
      b_s.reshape(A_GROUPS, A_CHUNK, 1).astype(F32))


def _dft_tables(n):
    idx = lax.iota(jnp.int32, n)
    prod = (idx[:, None] * idx[None, :]) % n
    theta = prod.astype(F32) * (2.0 * math.pi / n)
    scale = 1.0 / math.sqrt(n)
    return jnp.cos(theta) * scale, -jnp.sin(theta) * scale


def _half_channel_dft(gc):
    cos, nsin = _dft_tables(gc)
    half = gc // 2
    return jnp.concatenate([cos[:, :half + 1], -nsin[:, 1:half]], axis=1)


def _fourier_channel_order(gc):
    half = gc // 2
    return list(range(half + 1)) + list(range(gc - 1, half, -1))


def fourier_row_order(w_out, width):
    n = w_out.shape[1]
    gc = width // B_GROUPS
    order = jnp.asarray(_fourier_channel_order(gc), jnp.int32)
    onehot = (order[:, None] == lax.iota(jnp.int32, gc)[None, :]).astype(BF16)
    bn = _pick(n, (1024, 512))
    return pl.pallas_call(
        _chan_dft_kernel,
        grid=(B_GROUPS, n // bn),
        in_specs=[pl.BlockSpec((gc, gc), lambda g, j: (0, 0)),
                  pl.BlockSpec((gc, bn), lambda g, j: (g, j))],
        out_specs=pl.BlockSpec((gc, bn), lambda g, j: (g, j)),
        out_shape=jax.ShapeDtypeStruct((width, n), BF16),
        compiler_params=_params("parallel", "parallel"),
        name="fourier_row_order",
    )(onehot, w_out)


def _chan_dft_kernel(z_ref, w_ref, o_ref):
    o_ref[...] = jnp.dot(z_ref[...], w_ref[...], preferred_element_type=F32).astype(o_ref.dtype)


def fold_channel_dft(w_in, width):
    k = w_in.shape[0]
    gc = width // B_GROUPS
    bm = _pick(k, (1024, 512, 256))
    return pl.pallas_call(
        _chan_dft_kernel,
        grid=(k // bm, B_GROUPS),
        in_specs=[pl.BlockSpec((bm, gc), lambda i, g: (i, g)),
                  pl.BlockSpec((gc, gc), lambda i, g: (0, 0))],
        out_specs=pl.BlockSpec((bm, gc), lambda i, g: (i, g)),
        out_shape=jax.ShapeDtypeStruct((k, width), BF16),
        compiler_params=_params("parallel", "parallel"),
        name="fold_channel_dft",
    )(w_in, _half_channel_dft(gc).astype(BF16))


def _pos_dft_kernel(c_ref, ns_ref, pc_ref, pq_ref, o_ref):
    half = pc_ref.shape[1]
    a = jnp.dot(c_ref[...], pc_ref[...], preferred_element_type=F32)
    b = jnp.dot(ns_ref[...], pq_ref[...], preferred_element_type=F32)
    nyq = jnp.dot(c_ref[...], pq_ref[:, :V7X_LANES], preferred_element_type=F32)[:, 0:1]
    first = lax.broadcasted_iota(jnp.int32, a.shape, 1) == 0
    o_ref[:, :half] = (a + jnp.where(first, 0.0, b)).astype(o_ref.dtype)
    o_ref[:, half:] = jnp.where(first, nyq, a - b).astype(o_ref.dtype)


def fourier_positions(p, n_seq, seq):
    m, width = p.shape
    gc = width // B_GROUPS
    half = gc // 2
    cos, nsin = _dft_tables(seq)
    bo = _pick(seq, (1024, 512, 256))
    nb = seq // bo
    return pl.pallas_call(
        _pos_dft_kernel,
        grid=(n_seq, B_GROUPS, nb),
        in_specs=[pl.BlockSpec((bo, seq), lambda b, g, i: (i, 0)),
                  pl.BlockSpec((bo, seq), lambda b, g, i: (i, 0)),
                  pl.BlockSpec((seq, half), lambda b, g, i: (b, 2 * g)),
                  pl.BlockSpec((seq, half), lambda b, g, i: (b, 2 * g + 1))],
        out_specs=pl.BlockSpec((bo, gc), lambda b, g, i: (b * nb + i, g)),
        out_shape=jax.ShapeDtypeStruct((m, width), BF16),
        compiler_params=_params("parallel", "parallel", "arbitrary"),
        name="fourier_positions",
    )(cos.astype(BF16), nsin.astype(BF16), p, p)


def _head_norm(x, g):
    x = x.astype(F32)
    ms = jnp.mean(x * x, axis=-1, keepdims=True)
    return (x * lax.rsqrt(ms + EPS)) * g


def _key_pad(n_blocks, radius):
    return 0 if n_blocks == 1 else radius


def _band_bias(n_blocks, qblk, radius, dilation, slope, length):
    pad = _key_pad(n_blocks, radius)
    kw = qblk + 2 * pad
    row = lax.broadcasted_iota(jnp.int32, (qblk, kw), 0)
    col = lax.broadcasted_iota(jnp.int32, (qblk, kw), 1)
    dist = jnp.abs(row - (col - pad))
    inner = jnp.where(dist <= radius, -slope * (dist * dilation).astype(F32), NEG_INF)
    if n_blocks == 1:
        return inner[None]
    first = jnp.where(col < pad, NEG_INF, inner)
    last = jnp.where(col >= length - (n_blocks - 1) * qblk + pad, NEG_INF, inner)
    return jnp.stack([first] + [inner] * (n_blocks - 2) + [last])


def _windows(x, n_blocks, qblk, radius):
    pad = _key_pad(n_blocks, radius)
    if pad == 0:
        return x[None]
    zeros = jnp.zeros((pad, x.shape[1]), x.dtype)
    xp = jnp.concatenate([zeros, x, zeros], axis=0)
    return jnp.stack([xp[t * qblk:t * qblk + qblk + 2 * pad] for t in range(n_blocks)])


def _block_softmax_pv(q, k, v, bias):
    scores = jnp.einsum("bqc,bkc->bqk", q, k, preferred_element_type=F32) * (HEAD_DIM ** -0.5)
    logits = scores + bias
    mx = jnp.max(logits, axis=-1, keepdims=True)
    p = jnp.exp(logits - mx)
    den = jnp.sum(p, axis=-1, keepdims=True)
    pv = jnp.einsum("bqk,bkc->bqc", p.astype(BF16), v, preferred_element_type=F32)
    return mx, den, pv


def _dil_attn_kernel(*refs, seq):
    qkv_refs = refs[:3 * C_GROUPS]
    qg_ref, kg_ref, sl_ref, o_ref, q_s, k_s, v_s, m_s, l_s, acc_s = refs[3 * C_GROUPS:]
    slope = sl_ref[:, 0:1]
    lanes = (HEAD_DIM,)
    for g, (window, dilation) in enumerate(C_PAIRS):
        radius = window // (2 * dilation)
        length = seq // dilation
        qblk = min(length, 128)
        nb = length // qblk
        q_ref, k_ref, v_ref = qkv_refs[3 * g:3 * g + 3]
        qn = _head_norm(q_ref[...], qg_ref[g:g + 1, :])
        kn = _head_norm(k_ref[...], kg_ref[g:g + 1, :])

        def residue(ref, r):
            return ref[pl.ds(r, length, stride=dilation), :]

        if dilation == 1:
            qr, kr, vr = [qn.astype(BF16)], [kn.astype(BF16)], [v_ref[...]]
        else:
            q_s[...] = qn
            k_s[...] = kn
            v_s[...] = v_ref[...].astype(F32)
            qr = [residue(q_s, r).astype(BF16) for r in range(dilation)]
            kr = [residue(k_s, r).astype(BF16) for r in range(dilation)]
            vr = [residue(v_s, r).astype(BF16) for r in range(dilation)]
        qb = jnp.concatenate([x.reshape(nb, qblk, HEAD_DIM) for x in qr])
        kb = jnp.concatenate([_windows(x, nb, qblk, radius) for x in kr])
        vb = jnp.concatenate([_windows(x, nb, qblk, radius) for x in vr])
        bias = _band_bias(nb, qblk, radius, dilation, slope, length)
        bias = jnp.concatenate([bias] * dilation)
        mx, den, pv = _block_softmax_pv(qb, kb, vb, bias)
        mx = jnp.broadcast_to(mx, mx.shape[:2] + lanes)
        den = jnp.broadcast_to(den, den.shape[:2] + lanes)
        if g == 0:
            m_s[...] = mx.reshape(seq, HEAD_DIM)
            l_s[...] = den.reshape(seq, HEAD_DIM)
            acc_s[...] = pv.reshape(seq, HEAD_DIM)
            continue
        for r in range(dilation):
            rows = slice(r * nb, (r + 1) * nb)
            m_old = residue(m_s, r)
            m_blk = mx[rows].reshape(length, HEAD_DIM)
            m_new = jnp.maximum(m_old, m_blk)
            a = jnp.exp(m_old - m_new)
            b = jnp.exp(m_blk - m_new)
            l_new = a * residue(l_s, r) + b * den[rows].reshape(length, HEAD_DIM)
            acc_new = a * residue(acc_s, r) + b * pv[rows].reshape(length, HEAD_DIM)
            m_s[pl.ds(r, length, stride=dilation), :] = m_new
            l_s[pl.ds(r, length, stride=dilation), :] = l_new
            acc_s[pl.ds(r, length, stride=dilation), :] = acc_new
    o_ref[...] = (acc_s[...] / l_s[...]).astype(o_ref.dtype)


def dilated_mixture(z, q_gain, k_gain, n_seq, seq):
    for window, dilation in C_PAIRS:
        assert seq % (dilation * (window // (2 * dilation))) == 0
    slopes = 2.0 ** (-ALIBI_MAX * jnp.arange(1, C_HEADS + 1, dtype=F32) / C_HEADS)
    slopes = jnp.repeat(slopes, HEAD_DIM).reshape(1, C_WIDTH)
    heads_per_part = C_WIDTH // HEAD_DIM

    def qkv_spec(g, part):
        col = (g * 3 + part) * heads_per_part
        return pl.BlockSpec((seq, HEAD_DIM), lambda b, h: (b, col + h))

    gain_spec = pl.BlockSpec((C_GROUPS, HEAD_DIM), lambda b, h: (0, 0))
    state = pltpu.VMEM((seq, HEAD_DIM), F32)
    return pl.pallas_call(
        functools.partial(_dil_attn_kernel, seq=seq),
        grid=(n_seq, C_HEADS),
        in_specs=[qkv_spec(g, part) for g in range(C_GROUPS) for part in range(3)]
                 + [gain_spec, gain_spec, pl.BlockSpec((1, HEAD_DIM), lambda b, h: (0, h))],
        out_specs=pl.BlockSpec((seq, HEAD_DIM), lambda b, h: (b, h)),
        out_shape=jax.ShapeDtypeStruct((n_seq * seq, C_WIDTH), BF16),
        scratch_shapes=[state] * 6,
        compiler_params=_params("parallel", "parallel"),
        name="dilated_attention",
    )(*([z] * (3 * C_GROUPS)), q_gain.astype(F32), k_gain.astype(F32), slopes)


def _mem_attn_kernel(q_ref, kv_ref, qg_ref, kg_ref, o_ref):
    def heads(ref, c0):
        return [ref[:, c0 + h * HEAD_DIM:c0 + (h + 1) * HEAD_DIM] for h in range(MEM_HEADS)]

    qn = jnp.stack([_head_norm(x, qg_ref[...]).astype(BF16) for x in heads(q_ref, 0)])
    kn = jnp.stack([_head_norm(x, kg_ref[...]).astype(BF16) for x in heads(kv_ref, 0)])
    v = jnp.stack([x.astype(BF16) for x in heads(kv_ref, MEM_WIDTH)])
    s = jnp.einsum("hqc,hkc->hqk", qn, kn, preferred_element_type=F32) * (HEAD_DIM ** -0.5)
    mx = jnp.max(s, axis=-1, keepdims=True)
    e = jnp.exp(s - mx)
    p = e / jnp.sum(e, axis=-1, keepdims=True)
    o = jnp.einsum("hqk,hkc->hqc", p.astype(BF16), v, preferred_element_type=F32)
    for h in range(MEM_HEADS):
        o_ref[:, h * HEAD_DIM:(h + 1) * HEAD_DIM] = o[h].astype(o_ref.dtype)


def memory_attention(q, kv, q_gain, k_gain, n_seq, seq, n_mem):
    bm = _pick(seq, (1024, 512, 256))
    nb = seq // bm
    gain_spec = pl.BlockSpec((1, HEAD_DIM), lambda b, i: (0, 0))
    return pl.pallas_call(
        _mem_attn_kernel,
        grid=(n_seq, nb),
        in_specs=[pl.BlockSpec((bm, MEM_WIDTH), lambda b, i: (b * nb + i, 0)),
                  pl.BlockSpec((n_mem, 2 * MEM_WIDTH), lambda b, i: (b, 0)),
                  gain_spec, gain_spec],
        out_specs=pl.BlockSpec((bm, MEM_WIDTH), lambda b, i: (b * nb + i, 0)),
        out_shape=jax.ShapeDtypeStruct((n_seq * seq, MEM_WIDTH), BF16),
        compiler_params=_params("parallel", "arbitrary"),
        name="memory_attention",
    )(q, kv, q_gain.reshape(1, HEAD_DIM).astype(F32), k_gain.reshape(1, HEAD_DIM).astype(F32))


def _trunk(xs, mems, p, depth):
    seq, d = xs[0].shape[1:]
    n_mem = mems[0].shape[1]
    n_seq = sum(x.shape[0] for x in xs)
    res = [x.reshape(-1, d) for x in xs]
    xb, ssq = cast_rows(res)
    memb, mem_ssq = cast_rows([m.reshape(-1, d) for m in mems])
    w_ff1 = None
    for i in range(depth):
        kind, j = i % N_MIXERS, i // N_MIXERS
        w_in, w_out = p["w_in"][kind], p["w_out"][kind]
        y_width = w_in.shape[2] - MEM_WIDTH
        z, q_mem = matmul([xb], [p["b_w_in_dft"] if kind == 1 else w_in], layer=j, out_dtypes=[BF16],
                          n_cols=y_width, norm=True, ssq=ssq, act="gelu" if kind == 0 else None,
                          side=(w_in, y_width, MEM_WIDTH), name="in_proj")
        if kind == 0:
            y = gated_chunk_mlp(z, p["a_w_s"][j], p["a_b_s"][j], p["a_v_gain"][j])
        elif kind == 1:
            y = fourier_positions(z, n_seq, seq)
        else:
            y = dilated_mixture(z, p["c_q_gain"][j], p["c_k_gain"][j], n_seq, seq)
        kv = matmul([memb], [p["mem_w_kv"]], layer=i, out_dtypes=[F32], norm=True, ssq=mem_ssq, name="mem_kv")
        mo = memory_attention(q_mem, kv, p["mem_q_gain"][i], p["mem_k_gain"][i], n_seq, seq, n_mem)
        outs = matmul([y, mo], [p["b_w_out_y"] if kind == 1 else w_out, w_out], layer=j, out_dtypes=[F32, BF16],
                      row0s=[0, y.shape[1]], res=res, ssq_out=True, name="out_proj",
                      casts=[(p["w_ff1"], 0, p["ffn_gain"])] if i == 0 else ())
        x, xb, ssq = outs[0], outs[1], outs[-1]
        if i == 0:
            w_ff1 = outs[2]
        riders = [(p["w_ff2"], i, None)] + ([(p["w_ff1"], i + 1, p["ffn_gain"])] if i + 1 < depth else [])
        outs = matmul([xb], [w_ff1[None]], out_dtypes=[BF16], norm=True, ssq=ssq, act="relu2", casts=riders,
                      name="ffn_up")
        f, w_ff2 = outs[0], outs[1][None]
        if i + 1 < depth:
            w_ff1 = outs[2]
            x, xb, ssq = matmul_kred_res(f, w_ff2, x, layer=0, with_bf16=True, name="ffn_down")
            res = [x]
    ys, row0 = [], 0
    for t in xs:
        rows = t.shape[0] * seq
        y = matmul_kred_res(f, w_ff2, x, layer=0, with_bf16=False, row0=row0, rows=rows, name="ffn_down")
        ys.append(y.reshape(t.shape))
        row0 += rows
    return ys


def _fold_gain(w, g):
    return (g.astype(F32)[:, :, None] * w).astype(BF16)


@jax.jit
def kernel(x_prompt, x_sample, mem_prompt, mem_sample, mixer_norm, mem_norm, ffn_norm, mem_w_kv,
           mem_q_gain, mem_k_gain, w_ff1, w_ff2, a_w_in, a_w_out, a_w_s, a_b_s, a_v_gain,
           b_w_in, b_w_out, c_w_in, c_w_out, c_q_gain, c_k_gain):
    depth = mixer_norm.shape[0]
    w_in = [_fold_gain(w, mixer_norm[k::N_MIXERS]) for k, w in enumerate((a_w_in, b_w_in, c_w_in))]
    b_width = b_w_in.shape[2] - MEM_WIDTH
    w_out = [w.astype(BF16) for w in (a_w_out, b_w_out, c_w_out)]
    p = {
        "mem_q_gain": mem_q_gain, "mem_k_gain": mem_k_gain,
        "a_w_s": a_w_s, "a_b_s": a_b_s, "a_v_gain": a_v_gain,
        "c_q_gain": c_q_gain, "c_k_gain": c_k_gain,
        "mem_w_kv": _fold_gain(mem_w_kv, mem_norm),
        "w_ff1": w_ff1, "w_ff2": w_ff2, "ffn_gain": ffn_norm.astype(F32)[:, :, None],
        "w_in": w_in,
        "b_w_in_dft": jnp.stack([fold_channel_dft(w, b_width) for w in w_in[1]]),
        "w_out": w_out,
        "b_w_out_y": jnp.stack([fourier_row_order(w, b_width) for w in w_out[1]]),
    }
    y_prompt, y_sample = _trunk([x_prompt, x_sample], [mem_prompt, mem_sample], p, depth)
    return y_prompt, y_sample
```

```python
import functools
import math

import jax
import jax.numpy as jnp
from jax import lax
from jax.experimental import pallas as pl
from jax.experimental.pallas import tpu as pltpu

F32 = jnp.float32
BF16 = jnp.bfloat16

EPS = 1e-6
NEG_INF = -1e30
HEAD_DIM = 128
MEM_HEADS = 4
MEM_WIDTH = MEM_HEADS * HEAD_DIM
A_CHUNK = 128
A_GROUPS = 8
B_GROUPS = 4
C_PAIRS = ((128, 1), (512, 4), (2048, 16))
C_GROUPS = len(C_PAIRS)
C_HEADS = 8
C_WIDTH = C_HEADS * HEAD_DIM
ALIBI_MAX = 8.0
N_MIXERS = 3

V7X_VMEM_BYTES = 64 * 1024 * 1024
VMEM_LIMIT_BYTES = V7X_VMEM_BYTES - 2 * 1024 * 1024
V7X_LANES = 128


def _params(*semantics):
    return pltpu.CompilerParams(dimension_semantics=semantics, vmem_limit_bytes=VMEM_LIMIT_BYTES)


def _pick(n, candidates):
    for c in candidates:
        if n % c == 0:
            return c
    raise ValueError(f"no block size in {candidates} divides {n}")


def _gelu(x):
    return 0.5 * x * (1.0 + lax.erf(x * (1.0 / math.sqrt(2.0))))


def _stacked_specs(arrays, bm, width, axis_count):
    starts, specs, start = [], [], 0
    for a in arrays:
        nblk = a.shape[0] // bm
        assert a.shape[0] % bm == 0

        def index(i, *rest, start=start, nblk=nblk):
            return jnp.clip(i - start, 0, nblk - 1), rest[0] if axis_count > 1 else 0

        specs.append(pl.BlockSpec((bm, width), index))
        starts.append(start)
        start += nblk
    return specs, starts


def _select_stacked(refs, starts):
    i = pl.program_id(0)
    x = refs[0][...]
    for ref, start in zip(refs[1:], starts[1:]):
        x = jnp.where(i >= start, ref[...], x)
    return x


def _cast_rows_kernel(*refs, starts):
    xb_ref, ssq_ref = refs[-2:]
    x = _select_stacked(refs[:-2], starts)
    xb_ref[...] = x.astype(xb_ref.dtype)
    ssq_ref[...] = _lane_group_sumsq(x)


def cast_rows(xs):
    d = xs[0].shape[1]
    m = sum(x.shape[0] for x in xs)
    bm = _pick(math.gcd(*[x.shape[0] for x in xs]), (512, 256, 128, 8))
    in_specs, starts = _stacked_specs(xs, bm, d, 1)
    return pl.pallas_call(
        functools.partial(_cast_rows_kernel, starts=starts),
        grid=(m // bm,),
        in_specs=in_specs,
        out_specs=[pl.BlockSpec((bm, d), lambda i: (i, 0)), pl.BlockSpec((bm, V7X_LANES), lambda i: (i, 0))],
        out_shape=[jax.ShapeDtypeStruct((m, d), BF16), jax.ShapeDtypeStruct((m, V7X_LANES), F32)],
        compiler_params=_params("arbitrary"),
        name="cast_rows",
    )(*xs)


def _row_inv_rms(a_ref):
    rows, k = a_ref.shape
    part = jnp.zeros((rows, V7X_LANES), F32)
    for c in range(k // V7X_LANES):
        blk = a_ref[:, c * V7X_LANES:(c + 1) * V7X_LANES].astype(F32)
        part = part + blk * blk
    return lax.rsqrt(jnp.sum(part, axis=-1, keepdims=True) / k + EPS)


def _lane_group_sumsq(x):
    sq = x * x
    part = sq[:, :V7X_LANES]
    for c in range(1, x.shape[1] // V7X_LANES):
        part = part + sq[:, c * V7X_LANES:(c + 1) * V7X_LANES]
    return part


def _accumulate_over_j(ref, part, first):
    @pl.when(first)
    def _():
        ref[...] = part

    @pl.when(jnp.logical_not(first))
    def _():
        ref[...] += part


def _mm_kernel(*refs, n_ops, norm, act, res_starts, n_out, side, casts, ssq_out):
    a_refs = refs[:n_ops]
    w_refs = refs[n_ops:2 * n_ops]
    pos = 2 * n_ops
    ws_ref = refs[pos] if side else None
    pos += side
    ssq_in_ref = refs[pos] if norm else None
    pos += norm
    res_refs = refs[pos:pos + len(res_starts)]
    pos += len(res_starts)
    cast_in = []
    for has_gain, _ in casts:
        cast_in.append(refs[pos:pos + 1 + has_gain])
        pos += 1 + has_gain
    o_refs = refs[pos:pos + n_out]
    pos += n_out
    os_ref = refs[pos] if side else None
    pos += side
    step = pl.program_id(0) * pl.num_programs(1) + pl.program_id(1)
    for (src_ref, *gain_ref), (_, n_tiles) in zip(cast_in, casts):
        @pl.when(step < n_tiles)
        def _(src_ref=src_ref, gain_ref=gain_ref, dst_ref=refs[pos]):
            src = src_ref[...]
            dst_ref[...] = (src * gain_ref[0][...] if gain_ref else src).astype(BF16)
        pos += 1
    ssq_out_ref = refs[pos] if ssq_out else None
    pos += ssq_out
    first_col = pl.program_id(1) == 0
    if norm:
        inv_ref = refs[pos]

        @pl.when(first_col)
        def _():
            ms = jnp.sum(ssq_in_ref[...], axis=-1, keepdims=True) / a_refs[0].shape[1]
            inv = lax.rsqrt(ms + EPS)
            inv_ref[...] = jnp.broadcast_to(inv, inv_ref.shape)
            if side:
                d = jnp.dot(a_refs[0][...], ws_ref[...], preferred_element_type=F32)
                os_ref[...] = (d * inv).astype(os_ref.dtype)

    acc = None
    for a_ref, w_ref in zip(a_refs, w_refs):
        d = jnp.dot(a_ref[...], w_ref[...], preferred_element_type=F32)
        acc = d if acc is None else acc + d
    if norm:
        acc = acc * inv_ref[:, 0:1]
    if act == "relu2":
        r = jnp.maximum(acc, 0.0)
        acc = r * r
    elif act == "gelu":
        acc = _gelu(acc)
    if res_refs:
        acc = _select_stacked(res_refs, res_starts) + acc
    for o_ref in o_refs:
        o_ref[...] = acc.astype(o_ref.dtype)
    if ssq_out:
        _accumulate_over_j(ssq_out_ref, _lane_group_sumsq(acc), first_col)


CAST_TILE_COLS = 1024


def _cast_tiles(steps, k, n):
    ntc = n // CAST_TILE_COLS
    ntr = 1
    while ntr * 2 * ntc <= steps and k % (ntr * 2) == 0:
        ntr *= 2
    assert ntr * ntc <= steps and n % CAST_TILE_COLS == 0
    return ntr, ntc


def _mm_blocks(m, n, k_total, out_bytes, n_res, side_cols, cast_shapes):
    for bm, bn in ((1024, 1024), (1024, 512), (512, 1024), (512, 512), (256, 512)):
        if m % bm or n % bn:
            continue
        windows = 2 * (bm * k_total * 2 + k_total * bn * 2 + bm * bn * (out_bytes + 4 * n_res)
                       + (k_total + bm) * side_cols * 2)
        for ck, cn in cast_shapes:
            ntr, _ = _cast_tiles((m // bm) * (n // bn), ck, cn)
            windows += 2 * (ck // ntr) * (CAST_TILE_COLS * 6 + V7X_LANES * 4)
        if windows + 3 * bm * bn * 4 <= VMEM_LIMIT_BYTES:
            return bm, bn
    raise ValueError(f"no matmul blocks fit m={m} n={n} k={k_total}")


def matmul(a_list, w_list, *, out_dtypes, layer=0, n_cols=None, col0=0, row0s=None, norm=False, act=None,
           res=(), side=None, casts=(), ssq=None, ssq_out=False, name="matmul"):
    m = a_list[0].shape[0]
    n = n_cols if n_cols is not None else w_list[0].shape[2]
    row0s = row0s or [0] * len(a_list)
    ks = [a.shape[1] for a in a_list]
    out_bytes = sum(jnp.dtype(d).itemsize for d in out_dtypes)
    side_cols = side[2] if side else 0
    bm, bn = _mm_blocks(m, n, sum(ks), out_bytes, len(res), side_cols, [c[0].shape[1:] for c in casts])
    assert col0 % bn == 0 and all(r % k == 0 for r, k in zip(row0s, ks))
    assert not side or (norm and len(a_list) == 1 and side[1] % side[2] == 0)
    assert norm == (ssq is not None)
    cb = col0 // bn
    in_specs = [pl.BlockSpec((bm, k), lambda i, j: (i, 0)) for k in ks]
    in_specs += [pl.BlockSpec((None, k, bn), functools.partial(lambda i, j, rb: (layer, rb, cb + j), rb=r // k))
                 for r, k in zip(row0s, ks)]
    args = list(a_list) + list(w_list)
    out_specs = [pl.BlockSpec((bm, bn), lambda i, j: (i, j))] * len(out_dtypes)
    out_shape = [jax.ShapeDtypeStruct((m, n), d) for d in out_dtypes]
    if side:
        sb = side[1] // side_cols
        in_specs.append(pl.BlockSpec((None, ks[0], side_cols), lambda i, j: (layer, 0, sb)))
        args.append(side[0])
        out_specs.append(pl.BlockSpec((bm, side_cols), lambda i, j: (i, 0)))
        out_shape.append(jax.ShapeDtypeStruct((m, side_cols), BF16))
    ssq_spec = pl.BlockSpec((bm, V7X_LANES), lambda i, j: (i, 0))
    if norm:
        in_specs.append(ssq_spec)
        args.append(ssq)
    res_specs, res_starts = _stacked_specs(res, bm, bn, 2)
    in_specs += res_specs
    args += list(res)
    nj = n // bn
    cast_kinds = []
    for src, src_layer, gain in casts:
        ck, cn = src.shape[1:]
        ntr, ntc = _cast_tiles((m // bm) * nj, ck, cn)
        tr = ck // ntr

        def tile(i, j, ntr=ntr, ntc=ntc):
            t = jnp.minimum(i * nj + j, ntr * ntc - 1)
            return t // ntc, t % ntc

        in_specs.append(pl.BlockSpec((None, tr, CAST_TILE_COLS),
                                     functools.partial(lambda i, j, l, tile: (l, *tile(i, j)), l=src_layer, tile=tile)))
        args.append(src)
        if gain is not None:
            in_specs.append(pl.BlockSpec((None, tr, 1),
                                         functools.partial(lambda i, j, l, tile: (l, tile(i, j)[0], 0),
                                                           l=src_layer, tile=tile)))
            args.append(gain)
        out_specs = out_specs + [pl.BlockSpec((tr, CAST_TILE_COLS), tile)]
        out_shape = out_shape + [jax.ShapeDtypeStruct((ck, cn), BF16)]
        cast_kinds.append((gain is not None, ntr * ntc))
    if ssq_out:
        out_specs = out_specs + [ssq_spec]
        out_shape = out_shape + [jax.ShapeDtypeStruct((m, V7X_LANES), F32)]
    outs = pl.pallas_call(
        functools.partial(_mm_kernel, n_ops=len(a_list), norm=norm, act=act, res_starts=tuple(res_starts),
                          n_out=len(out_dtypes), side=bool(side),
                          casts=tuple(cast_kinds), ssq_out=ssq_out),
        grid=(m // bm, n // bn),
        in_specs=in_specs,
        out_specs=out_specs,
        out_shape=out_shape,
        scratch_shapes=[pltpu.VMEM((bm, V7X_LANES), F32)] if norm else [],
        compiler_params=_params("parallel", "arbitrary"),
        name=name,
    )(*args)
    return outs if len(outs) > 1 else outs[0]


def _mm_kred_kernel(a_ref, w_ref, res_ref, o_ref, *next_layer_refs):
    kk = pl.program_id(2)

    def dot():
        return jnp.dot(a_ref[...], w_ref[...], preferred_element_type=F32)

    @pl.when(kk == 0)
    def _():
        o_ref[...] = res_ref[...] + dot()

    @pl.when(kk > 0)
    def _():
        o_ref[...] += dot()

    if next_layer_refs:
        ob_ref, ssq_ref = next_layer_refs
        last = kk == pl.num_programs(2) - 1
        first_col = pl.program_id(1) == 0

        @pl.when(last)
        def _():
            ob_ref[...] = o_ref[...].astype(ob_ref.dtype)

        @pl.when(jnp.logical_and(last, first_col))
        def _():
            ssq_ref[...] = _lane_group_sumsq(o_ref[...])

        @pl.when(jnp.logical_and(last, jnp.logical_not(first_col)))
        def _():
            ssq_ref[...] += _lane_group_sumsq(o_ref[...])


def matmul_kred_res(a, w, res, *, layer, with_bf16, row0=0, rows=None, name="matmul_kred"):
    k = a.shape[1]
    m = rows if rows is not None else a.shape[0]
    n = w.shape[2]
    bm = _pick(math.gcd(m, row0) if row0 else m, (1024, 512, 256))
    bn = _pick(n, (1024, 512))
    bk = _pick(k, (4096, 2048, 1024))
    rb0 = row0 // bm
    out_spec = pl.BlockSpec((bm, bn), lambda i, j, kk: (i, j))
    out_shape = [jax.ShapeDtypeStruct((m, n), F32)]
    out_specs = [out_spec]
    if with_bf16:
        out_shape += [jax.ShapeDtypeStruct((m, n), BF16), jax.ShapeDtypeStruct((m, V7X_LANES), F32)]
        out_specs += [out_spec, pl.BlockSpec((bm, V7X_LANES), lambda i, j, kk: (i, 0))]
    outs = pl.pallas_call(
        _mm_kred_kernel,
        grid=(m // bm, n // bn, k // bk),
        in_specs=[pl.BlockSpec((bm, bk), lambda i, j, kk: (rb0 + i, kk)),
                  pl.BlockSpec((None, bk, bn), lambda i, j, kk: (layer, kk, j)),
                  pl.BlockSpec((bm, bn), lambda i, j, kk: (rb0 + i, j))],
        out_specs=out_specs,
        out_shape=out_shape,
        compiler_params=_params("parallel", "arbitrary", "arbitrary"),
        name=name,
    )(a, w, res)
    return outs if with_bf16 else outs[0]


def _gate_kernel(u_ref, v_ref, vg_ref, ws_ref, bs_ref, o_ref, *, n_chunks, gw):
    inv = _row_inv_rms(v_ref)

    def gate(g, carry):
        c0 = pl.multiple_of(g * gw, gw)
        vn = ((v_ref[:, pl.ds(c0, gw)].astype(F32) * inv) * vg_ref[:, pl.ds(c0, gw)]).astype(BF16)
        u = u_ref[:, pl.ds(c0, gw)].astype(F32)
        w = ws_ref[g]
        b = bs_ref[g]
        for c in range(n_chunks):
            r0 = c * A_CHUNK
            vm = jnp.dot(w, vn[r0:r0 + A_CHUNK], preferred_element_type=F32) + b
            o_ref[r0:r0 + A_CHUNK, pl.ds(c0, gw)] = (u[r0:r0 + A_CHUNK] * vm).astype(o_ref.dtype)
        return carry

    lax.fori_loop(0, A_GROUPS, gate, 0)


def gated_chunk_mlp(uv, w_s, b_s, v_gain):
    m, w2 = uv.shape
    width = w2 // 2
    gw = width // A_GROUPS
    rows = _pick(m, (512, 256, 128))
    return pl.pallas_call(
        functools.partial(_gate_kernel, n_chunks=rows // A_CHUNK, gw=gw),
        grid=(m // rows,),
        in_specs=[pl.BlockSpec((rows, width), lambda i: (i, 0)),
                  pl.BlockSpec((rows, width), lambda i: (i, 1)),
                  pl.BlockSpec((1, width), lambda i: (0, 0)),
                  pl.BlockSpec((A_GROUPS, A_CHUNK, A_CHUNK), lambda i: (0, 0, 0)),
                  pl.BlockSpec((A_GROUPS, A_CHUNK, 1), lambda i: (0, 0, 0))],
        out_specs=pl.BlockSpec((rows, width), lambda i: (i, 0)),
        out_shape=jax.ShapeDtypeStruct((m, width), BF16),
        compiler_params=_params("parallel"),
        name="gated_chunk_mlp",
    )(uv, uv, v_gain.reshape(1, width).astype(F32), w_s.astype(BF16),
      b_s.reshape(A_GROUPS, A_CHUNK, 1).astype(F32))


def _dft_tables(n):
    idx = lax.iota(jnp.int32, n)
    prod = (idx[:, None] * idx[None, :]) % n
    theta = prod.astype(F32) * (2.0 * math.pi / n)
    scale = 1.0 / math.sqrt(n)
    return jnp.cos(theta) * scale, -jnp.sin(theta) * scale


def _half_channel_dft(gc):
    cos, nsin = _dft_tables(gc)
    half = gc // 2
    return jnp.concatenate([cos[:, :half + 1], -nsin[:, 1:half]], axis=1)


def _fourier_channel_order(gc):
    half = gc // 2
    return list(range(half + 1)) + list(range(gc - 1, half, -1))


def fourier_row_order(w_out, width):
    n = w_out.shape[1]
    gc = width // B_GROUPS
    order = jnp.asarray(_fourier_channel_order(gc), jnp.int32)
    onehot = (order[:, None] == lax.iota(jnp.int32, gc)[None, :]).astype(BF16)
    bn = _pick(n, (1024, 512))
    return pl.pallas_call(
        _chan_dft_kernel,
        grid=(B_GROUPS, n // bn),
        in_specs=[pl.BlockSpec((gc, gc), lambda g, j: (0, 0)),
                  pl.BlockSpec((gc, bn), lambda g, j: (g, j))],
        out_specs=pl.BlockSpec((gc, bn), lambda g, j: (g, j)),
        out_shape=jax.ShapeDtypeStruct((width, n), BF16),
        compiler_params=_params("parallel", "parallel"),
        name="fourier_row_order",
    )(onehot, w_out)


def _chan_dft_kernel(z_ref, w_ref, o_ref):
    o_ref[...] = jnp.dot(z_ref[...], w_ref[...], preferred_element_type=F32).astype(o_ref.dtype)


def fold_channel_dft(w_in, width):
    k = w_in.shape[0]
    gc = width // B_GROUPS
    bm = _pick(k, (1024, 512, 256))
    return pl.pallas_call(
        _chan_dft_kernel,
        grid=(k // bm, B_GROUPS),
        in_specs=[pl.BlockSpec((bm, gc), lambda i, g: (i, g)),
                  pl.BlockSpec((gc, gc), lambda i, g: (0, 0))],
        out_specs=pl.BlockSpec((bm, gc), lambda i, g: (i, g)),
        out_shape=jax.ShapeDtypeStruct((k, width), BF16),
        compiler_params=_params("parallel", "parallel"),
        name="fold_channel_dft",
    )(w_in, _half_channel_dft(gc).astype(BF16))


def _pos_dft_kernel(c_ref, ns_ref, pc_ref, pq_ref, o_ref):
    half = pc_ref.shape[1]
    a = jnp.dot(c_ref[...], pc_ref[...], preferred_element_type=F32)
    b = jnp.dot(ns_ref[...], pq_ref[...], preferred_element_type=F32)
    nyq = jnp.dot(c_ref[...], pq_ref[:, :V7X_LANES], preferred_element_type=F32)[:, 0:1]
    first = lax.broadcasted_iota(jnp.int32, a.shape, 1) == 0
    o_ref[:, :half] = (a + jnp.where(first, 0.0, b)).astype(o_ref.dtype)
    o_ref[:, half:] = jnp.where(first, nyq, a - b).astype(o_ref.dtype)


def fourier_positions(p, n_seq, seq):
    m, width = p.shape
    gc = width // B_GROUPS
    half = gc // 2
    cos, nsin = _dft_tables(seq)
    bo = _pick(seq, (1024, 512, 256))
    nb = seq // bo
    return pl.pallas_call(
        _pos_dft_kernel,
        grid=(n_seq, B_GROUPS, nb),
        in_specs=[pl.BlockSpec((bo, seq), lambda b, g, i: (i, 0)),
                  pl.BlockSpec((bo, seq), lambda b, g, i: (i, 0)),
                  pl.BlockSpec((seq, half), lambda b, g, i: (b, 2 * g)),
                  pl.BlockSpec((seq, half), lambda b, g, i: (b, 2 * g + 1))],
        out_specs=pl.BlockSpec((bo, gc), lambda b, g, i: (b * nb + i, g)),
        out_shape=jax.ShapeDtypeStruct((m, width), BF16),
        compiler_params=_params("parallel", "parallel", "arbitrary"),
        name="fourier_positions",
    )(cos.astype(BF16), nsin.astype(BF16), p, p)


def _head_norm(x, g):
    x = x.astype(F32)
    ms = jnp.mean(x * x, axis=-1, keepdims=True)
    return (x * lax.rsqrt(ms + EPS)) * g


def _key_pad(n_blocks, radius):
    return 0 if n_blocks == 1 else radius


def _band_bias(n_blocks, qblk, radius, dilation, slope, length):
    pad = _key_pad(n_blocks, radius)
    kw = qblk + 2 * pad
    row = lax.broadcasted_iota(jnp.int32, (qblk, kw), 0)
    col = lax.broadcasted_iota(jnp.int32, (qblk, kw), 1)
    dist = jnp.abs(row - (col - pad))
    inner = jnp.where(dist <= radius, -slope * (dist * dilation).astype(F32), NEG_INF)
    if n_blocks == 1:
        return inner[None]
    first = jnp.where(col < pad, NEG_INF, inner)
    last = jnp.where(col >= length - (n_blocks - 1) * qblk + pad, NEG_INF, inner)
    return jnp.stack([first] + [inner] * (n_blocks - 2) + [last])


def _windows(x, n_blocks, qblk, radius):
    pad = _key_pad(n_blocks, radius)
    if pad == 0:
        return x[None]
    zeros = jnp.zeros((pad, x.shape[1]), x.dtype)
    xp = jnp.concatenate([zeros, x, zeros], axis=0)
    return jnp.stack([xp[t * qblk:t * qblk + qblk + 2 * pad] for t in range(n_blocks)])


def _block_softmax_pv(q, k, v, bias):
    scores = jnp.einsum("bqc,bkc->bqk", q, k, preferred_element_type=F32) * (HEAD_DIM ** -0.5)
    logits = scores + bias
    mx = jnp.max(logits, axis=-1, keepdims=True)
    p = jnp.exp(logits - mx)
    den = jnp.sum(p, axis=-1, keepdims=True)
    pv = jnp.einsum("bqk,bkc->bqc", p.astype(BF16), v, preferred_element_type=F32)
    return mx, den, pv


def _dil_attn_kernel(*refs, seq):
    qkv_refs = refs[:3 * C_GROUPS]
    qg_ref, kg_ref, sl_ref, o_ref, q_s, k_s, v_s, m_s, l_s, acc_s = refs[3 * C_GROUPS:]
    slope = sl_ref[:, 0:1]
    lanes = (HEAD_DIM,)
    for g, (window, dilation) in enumerate(C_PAIRS):
        radius = window // (2 * dilation)
        length = seq // dilation
        qblk = min(length, 128)
        nb = length // qblk
        q_ref, k_ref, v_ref = qkv_refs[3 * g:3 * g + 3]
        qn = _head_norm(q_ref[...], qg_ref[g:g + 1, :])
        kn = _head_norm(k_ref[...], kg_ref[g:g + 1, :])

        def residue(ref, r):
            return ref[pl.ds(r, length, stride=dilation), :]

        if dilation == 1:
            qr, kr, vr = [qn.astype(BF16)], [kn.astype(BF16)], [v_ref[...]]
        else:
            q_s[...] = qn
            k_s[...] = kn
            v_s[...] = v_ref[...].astype(F32)
            qr = [residue(q_s, r).astype(BF16) for r in range(dilation)]
            kr = [residue(k_s, r).astype(BF16) for r in range(dilation)]
            vr = [residue(v_s, r).astype(BF16) for r in range(dilation)]
        qb = jnp.concatenate([x.reshape(nb, qblk, HEAD_DIM) for x in qr])
        kb = jnp.concatenate([_windows(x, nb, qblk, radius) for x in kr])
        vb = jnp.concatenate([_windows(x, nb, qblk, radius) for x in vr])
        bias = _band_bias(nb, qblk, radius, dilation, slope, length)
        bias = jnp.concatenate([bias] * dilation)
        mx, den, pv = _block_softmax_pv(qb, kb, vb, bias)
        mx = jnp.broadcast_to(mx, mx.shape[:2] + lanes)
        den = jnp.broadcast_to(den, den.shape[:2] + lanes)
        if g == 0:
            m_s[...] = mx.reshape(seq, HEAD_DIM)
            l_s[...] = den.reshape(seq, HEAD_DIM)
            acc_s[...] = pv.reshape(seq, HEAD_DIM)
            continue
        for r in range(dilation):
            rows = slice(r * nb, (r + 1) * nb)
            m_old = residue(m_s, r)
            m_blk = mx[rows].reshape(length, HEAD_DIM)
            m_new = jnp.maximum(m_old, m_blk)
            a = jnp.exp(m_old - m_new)
            b = jnp.exp(m_blk - m_new)
            l_new = a * residue(l_s, r) + b * den[rows].reshape(length, HEAD_DIM)
            acc_new = a * residue(acc_s, r) + b * pv[rows].reshape(length, HEAD_DIM)
            m_s[pl.ds(r, length, stride=dilation), :] = m_new
            l_s[pl.ds(r, length, stride=dilation), :] = l_new
            acc_s[pl.ds(r, length, stride=dilation), :] = acc_new
    o_ref[...] = (acc_s[...] / l_s[...]).astype(o_ref.dtype)


def dilated_mixture(z, q_gain, k_gain, n_seq, seq):
    for window, dilation in C_PAIRS:
        assert seq % (dilation * (window // (2 * dilation))) == 0
    slopes = 2.0 ** (-ALIBI_MAX * jnp.arange(1, C_HEADS + 1, dtype=F32) / C_HEADS)
    slopes = jnp.repeat(slopes, HEAD_DIM).reshape(1, C_WIDTH)
    heads_per_part = C_WIDTH // HEAD_DIM

    def qkv_spec(g, part):
        col = (g * 3 + part) * heads_per_part
        return pl.BlockSpec((seq, HEAD_DIM), lambda b, h: (b, col + h))

    gain_spec = pl.BlockSpec((C_GROUPS, HEAD_DIM), lambda b, h: (0, 0))
    state = pltpu.VMEM((seq, HEAD_DIM), F32)
    return pl.pallas_call(
        functools.partial(_dil_attn_kernel, seq=seq),
        grid=(n_seq, C_HEADS),
        in_specs=[qkv_spec(g, part) for g in range(C_GROUPS) for part in range(3)]
                 + [gain_spec, gain_spec, pl.BlockSpec((1, HEAD_DIM), lambda b, h: (0, h))],
        out_specs=pl.BlockSpec((seq, HEAD_DIM), lambda b, h: (b, h)),
        out_shape=jax.ShapeDtypeStruct((n_seq * seq, C_WIDTH), BF16),
        scratch_shapes=[state] * 6,
        compiler_params=_params("parallel", "parallel"),
        name="dilated_attention",
    )(*([z] * (3 * C_GROUPS)), q_gain.astype(F32), k_gain.astype(F32), slopes)


def _mem_attn_kernel(q_ref, kv_ref, qg_ref, kg_ref, o_ref):
    def heads(ref, c0):
        return [ref[:, c0 + h * HEAD_DIM:c0 + (h + 1) * HEAD_DIM] for h in range(MEM_HEADS)]

    qn = jnp.stack([_head_norm(x, qg_ref[...]).astype(BF16) for x in heads(q_ref, 0)])
    kn = jnp.stack([_head_norm(x, kg_ref[...]).astype(BF16) for x in heads(kv_ref, 0)])
    v = jnp.stack([x.astype(BF16) for x in heads(kv_ref, MEM_WIDTH)])
    s = jnp.einsum("hqc,hkc->hqk", qn, kn, preferred_element_type=F32) * (HEAD_DIM ** -0.5)
    mx = jnp.max(s, axis=-1, keepdims=True)
    e = jnp.exp(s - mx)
    p = e / jnp.sum(e, axis=-1, keepdims=True)
    o = jnp.einsum("hqk,hkc->hqc", p.astype(BF16), v, preferred_element_type=F32)
    for h in range(MEM_HEADS):
        o_ref[:, h * HEAD_DIM:(h + 1) * HEAD_DIM] = o[h].astype(o_ref.dtype)


def memory_attention(q, kv, q_gain, k_gain, n_seq, seq, n_mem):
    bm = _pick(seq, (1024, 512, 256))
    nb = seq // bm
    gain_spec = pl.BlockSpec((1, HEAD_DIM), lambda b, i: (0, 0))
    return pl.pallas_call(
        _mem_attn_kernel,
        grid=(n_seq, nb),
        in_specs=[pl.BlockSpec((bm, MEM_WIDTH), lambda b, i: (b * nb + i, 0)),
                  pl.BlockSpec((n_mem, 2 * MEM_WIDTH), lambda b, i: (b, 0)),
                  gain_spec, gain_spec],
        out_specs=pl.BlockSpec((bm, MEM_WIDTH), lambda b, i: (b * nb + i, 0)),
        out_shape=jax.ShapeDtypeStruct((n_seq * seq, MEM_WIDTH), BF16),
        compiler_params=_params("parallel", "arbitrary"),
        name="memory_attention",
    )(q, kv, q_gain.reshape(1, HEAD_DIM).astype(F32), k_gain.reshape(1, HEAD_DIM).astype(F32))


def _trunk(xs, mems, p, depth):
    seq, d = xs[0].shape[1:]
    n_mem = mems[0].shape[1]
    n_seq = sum(x.shape[0] for x in xs)
    res = [x.reshape(-1, d) for x in xs]
    xb, ssq = cast_rows(res)
    memb, mem_ssq = cast_rows([m.reshape(-1, d) for m in mems])
    w_ff1 = None
    for i in range(depth):
        kind, j = i % N_MIXERS, i // N_MIXERS
        w_in, w_out = p["w_in"][kind], p["w_out"][kind]
        y_width = w_in.shape[2] - MEM_WIDTH
        z, q_mem = matmul([xb], [p["b_w_in_dft"] if kind == 1 else w_in], layer=j, out_dtypes=[BF16],
                          n_cols=y_width, norm=True, ssq=ssq, act="gelu" if kind == 0 else None,
                          side=(w_in, y_width, MEM_WIDTH), name="in_proj")
        if kind == 0:
            y = gated_chunk_mlp(z, p["a_w_s"][j], p["a_b_s"][j], p["a_v_gain"][j])
        elif kind == 1:
            y = fourier_positions(z, n_seq, seq)
        else:
            y = dilated_mixture(z, p["c_q_gain"][j], p["c_k_gain"][j], n_seq, seq)
        kv = matmul([memb], [p["mem_w_kv"]], layer=i, out_dtypes=[F32], norm=True, ssq=mem_ssq, name="mem_kv")
        mo = memory_attention(q_mem, kv, p["mem_q_gain"][i], p["mem_k_gain"][i], n_seq, seq, n_mem)
        outs = matmul([y, mo], [p["b_w_out_y"] if kind == 1 else w_out, w_out], layer=j, out_dtypes=[F32, BF16],
                      row0s=[0, y.shape[1]], res=res, ssq_out=True, name="out_proj",
                      casts=[(p["w_ff1"], 0, p["ffn_gain"])] if i == 0 else ())
        x, xb, ssq = outs[0], outs[1], outs[-1]
        if i == 0:
            w_ff1 = outs[2]
        riders = [(p["w_ff2"], i, None)] + ([(p["w_ff1"], i + 1, p["ffn_gain"])] if i + 1 < depth else [])
        outs = matmul([xb], [w_ff1[None]], out_dtypes=[BF16], norm=True, ssq=ssq, act="relu2", casts=riders,
                      name="ffn_up")
        f, w_ff2 = outs[0], outs[1][None]
        if i + 1 < depth:
            w_ff1 = outs[2]
            x, xb, ssq = matmul_kred_res(f, w_ff2, x, layer=0, with_bf16=True, name="ffn_down")
            res = [x]
    ys, row0 = [], 0
    for t in xs:
        rows = t.shape[0] * seq
        y = matmul_kred_res(f, w_ff2, x, layer=0, with_bf16=False, row0=row0, rows=rows, name="ffn_down")
        ys.append(y.reshape(t.shape))
        row0 += rows
    return ys


def _fold_gain(w, g):
    return (g.astype(F32)[:, :, None] * w).astype(BF16)


@jax.jit
def kernel(x_prompt, x_sample, mem_prompt, mem_sample, mixer_norm, mem_norm, ffn_norm, mem_w_kv,
           mem_q_gain, mem_k_gain, w_ff1, w_ff2, a_w_in, a_w_out, a_w_s, a_b_s, a_v_gain,
           b_w_in, b_w_out, c_w_in, c_w_out, c_q_gain, c_k_gain):
    depth = mixer_norm.shape[0]
    w_in = [_fold_gain(w, mixer_norm[k::N_MIXERS]) for k, w in enumerate((a_w_in, b_w_in, c_w_in))]
    b_width = b_w_in.shape[2] - MEM_WIDTH
    w_out = [w.astype(BF16) for w in (a_w_out, b_w_out, c_w_out)]
    p = {
        "mem_q_gain": mem_q_gain, "mem_k_gain": mem_k_gain,
        "a_w_s": a_w_s, "a_b_s": a_b_s, "a_v_gain": a_v_gain,
        "c_q_gain": c_q_gain, "c_k_gain": c_k_gain,
        "mem_w_kv": _fold_gain(mem_w_kv, mem_norm),
        "w_ff1": w_ff1, "w_ff2": w_ff2, "ffn_gain": ffn_norm.astype(F32)[:, :, None],
        "w_in": w_in,
        "b_w_in_dft": jnp.stack([fold_channel_dft(w, b_width) for w in w_in[1]]),
        "w_out": w_out,
        "b_w_out_y": jnp.stack([fourier_row_order(w, b_width) for w in w_out[1]]),
    }
    y_prompt, y_sample = _trunk([x_prompt, x_sample], [mem_prompt, mem_sample], p, depth)
    return y_prompt, y_sample
```
